```python
import math
import jax, jax.numpy as jnp
from jax import lax
import numpy as np

D_MODEL = 1024
BATCH = 8
SEQ = 2048
DEPTH = 4
DEC_BATCH = 32
DEC_SEQ = 8
PAST_LEN = 8192
PAGE_SIZE = 128

N_MIXERS = 2
N_ATTN_LAYERS = (DEPTH + 1) // 2
N_CONV_LAYERS = DEPTH // 2
N_HEADS = 8
DQK = D_MODEL // N_HEADS // 2
DV = 2 * DQK
N_BUCKETS = 32
MAX_DISTANCE = 128
Q_BLOCK = 128
CONV_W = 31
D_CONV = D_MODEL
D_FF = 2816
N_MEM = 256
N_XHEADS = 4
XHEAD_DIM = D_MODEL // N_XHEADS
EPS = 1e-6

kernel_name = 'diff_conformer_hybrid_step'


def rmsnorm(x, g):
    xf = x.astype(jnp.float32)
    y = xf * lax.rsqrt(jnp.mean(xf * xf, axis=-1, keepdims=True) + EPS)
    return (y * g.astype(jnp.float32)).astype(x.dtype)


def layernorm(x, g, b):
    xf = x.astype(jnp.float32)
    mu = jnp.mean(xf, axis=-1, keepdims=True)
    xc = xf - mu
    y = xc * lax.rsqrt(jnp.mean(xc * xc, axis=-1, keepdims=True) + EPS)
    return (y * g.astype(jnp.float32) + b.astype(jnp.float32)).astype(x.dtype)


def swiglu_ffn(x, g, w_in, w_out):
    a, u = jnp.split(rmsnorm(x, g) @ w_in, 2, axis=-1)
    return (jax.nn.silu(a) * u) @ w_out


def rel_bucket(q_pos, k_pos):
    n = jnp.maximum(q_pos[:, None] - k_pos[None, :], 0)
    max_exact = N_BUCKETS // 2
    nf = jnp.maximum(n, 1).astype(jnp.float32)
    large = max_exact + (jnp.log(nf / max_exact) / math.log(MAX_DISTANCE / max_exact)
                         * (N_BUCKETS - max_exact)).astype(jnp.int32)
    large = jnp.minimum(large, N_BUCKETS - 1)
    return jnp.where(n < max_exact, n, large)


def compute_lambda(lp, layer_idx):
    lam_init = 0.8 - 0.6 * math.exp(-0.3 * layer_idx)
    lpf = lp.astype(jnp.float32)
    lam = jnp.exp(jnp.sum(lpf[0] * lpf[1])) - jnp.exp(jnp.sum(lpf[2] * lpf[3])) + lam_init
    return lam, lam_init


def diff_qkv(h, w_qkv):
    B, T, _ = h.shape
    q, k, v = jnp.split(h @ w_qkv, 3, axis=-1)
    return (q.reshape(B, T, 2 * N_HEADS, DQK), k.reshape(B, T, 2 * N_HEADS, DQK),
            v.reshape(B, T, N_HEADS, DV))


def diff_attend(q, k, v, q_pos, k_pos, rel_bias, lam):
    B, Tq = q.shape[:2]
    Tk = k.shape[1]
    s = jnp.einsum('bqhd,bkhd->bhqk', q, k).astype(jnp.float32) * (DQK ** -0.5)
    bias = jnp.transpose(rel_bias[rel_bucket(q_pos, k_pos)], (2, 0, 1))
    bias = jnp.repeat(bias.astype(jnp.float32), 2, axis=0)
    s = s + bias[None]
    s = jnp.where((k_pos[None, :] <= q_pos[:, None])[None, None], s, -jnp.inf)
    p = jax.nn.softmax(s, axis=-1).reshape(B, N_HEADS, 2, Tq, Tk)
    a = p[:, :, 0] - lam * p[:, :, 1]
    return jnp.einsum('bhqk,bkhd->bqhd', a.astype(v.dtype), v)


def diff_attn_prompt(q, k, v, rel_bias, lam):
    B, T = q.shape[:2]
    nblk = T // Q_BLOCK
    k_pos = jnp.arange(T)
    qb = q.reshape(B, nblk, Q_BLOCK, 2 * N_HEADS, DQK).transpose(1, 0, 2, 3, 4)

    def one(args):
        i, qi = args
        q_pos = i * Q_BLOCK + jnp.arange(Q_BLOCK)
        return diff_attend(qi, k, v, q_pos, k_pos, rel_bias, lam)

    o = lax.map(one, (jnp.arange(nblk), qb))
    return o.transpose(1, 0, 2, 3, 4).reshape(B, T, N_HEADS, DV)


def diff_out(o, lam_init, subln, w_o):
    B, T = o.shape[:2]
    o = rmsnorm(o, subln) * (1.0 - lam_init)
    return o.reshape(B, T, N_HEADS * DV) @ w_o


def gather_pages(cache, page_table):
    g = cache[page_table]
    return g.reshape((g.shape[0], g.shape[1] * g.shape[2]) + g.shape[3:])


def conv_module(h, hist, w_pw1, b_pw1, w_dw, b_dw, ln_g, ln_b, w_pw2, b_pw2):
    a, gte = jnp.split(h @ w_pw1 + b_pw1, 2, axis=-1)
    u = a * jax.nn.sigmoid(gte)
    buf = jnp.concatenate([hist.astype(u.dtype), u], axis=1)
    y = lax.conv_general_dilated(buf, w_dw[:, None, :].astype(buf.dtype), window_strides=(1,),
                                 padding='VALID', dimension_numbers=('NWC', 'WIO', 'NWC'),
                                 feature_group_count=D_CONV) + b_dw
    y = jax.nn.silu(layernorm(y, ln_g, ln_b))
    return y @ w_pw2 + b_pw2, buf[:, -(CONV_W - 1):]


def mem_kv(mem, w_kv):
    B = mem.shape[0]
    k, v = jnp.split(mem @ w_kv, 2, axis=-1)
    return k.reshape(B, N_MEM, N_XHEADS, XHEAD_DIM), v.reshape(B, N_MEM, N_XHEADS, XHEAD_DIM)


def cross_attend(h, mk, mv, w_q, w_o):
    B, T, _ = h.shape
    q = (h @ w_q).reshape(B, T, N_XHEADS, XHEAD_DIM)
    s = jnp.einsum('bqhd,bkhd->bhqk', q, mk).astype(jnp.float32) * (XHEAD_DIM ** -0.5)
    p = jax.nn.softmax(s, axis=-1).astype(mv.dtype)
    return jnp.einsum('bhqk,bkhd->bqhd', p, mv).reshape(B, T, D_MODEL) @ w_o


def setup_inputs(seed: int = 0) -> dict:
    key = jax.random.key(seed)
    keys = jax.random.split(key, 48)
    ctr = [0]

    def nk():
        k = keys[ctr[0]]
        ctr[0] += 1
        return k

    def nrm(shape, scale=1.0):
        return jax.random.normal(nk(), shape, jnp.float32) * scale

    n_pages = PAST_LEN // PAGE_SIZE
    n_used = DEC_BATCH * n_pages
    n_pool = n_used + max(1, n_used // 4)
    perm = jax.random.permutation(nk(), n_pool)
    page_table = perm[:n_used].reshape(DEC_BATCH, n_pages).astype(jnp.int32)
    D = D_MODEL
    return {
        'x_prompt': nrm((BATCH, SEQ, D)),
        'x_sample': nrm((DEC_BATCH, DEC_SEQ, D)),
        'cache_k_diff': nrm((N_ATTN_LAYERS, n_pool, PAGE_SIZE, 2 * N_HEADS, DQK)),
        'cache_v_diff': nrm((N_ATTN_LAYERS, n_pool, PAGE_SIZE, N_HEADS, DV)),
        'state_conv': nrm((N_CONV_LAYERS, DEC_BATCH, CONV_W - 1, D_CONV), 0.5),
        'cache_mem_k': nrm((DEPTH, DEC_BATCH, N_MEM, N_XHEADS, XHEAD_DIM)),
        'cache_mem_v': nrm((DEPTH, DEC_BATCH, N_MEM, N_XHEADS, XHEAD_DIM)),
        'page_table': page_table,
        'mem_prompt': nrm((BATCH, N_MEM, D)),
        'ffn_norm': 1.0 + nrm((DEPTH, 2, D), 0.02),
        'ffn_w_in': nrm((DEPTH, 2, D, 2 * D_FF), D ** -0.5),
        'ffn_w_out': nrm((DEPTH, 2, D_FF, D), D_FF ** -0.5),
        'mix_norm': 1.0 + nrm((DEPTH, D), 0.02),
        'diff_w_qkv': nrm((N_ATTN_LAYERS, D, 3 * D), D ** -0.5),
        'diff_lambda': nrm((N_ATTN_LAYERS, 4, DQK), 0.1),
        'diff_subln': 1.0 + nrm((N_ATTN_LAYERS, DV), 0.02),
        'diff_w_o': nrm((N_ATTN_LAYERS, N_HEADS * DV, D), (N_HEADS * DV) ** -0.5),
        'rel_bias': nrm((N_BUCKETS, N_HEADS), 0.5),
        'conv_w_pw1': nrm((N_CONV_LAYERS, D, 2 * D_CONV), D ** -0.5),
        'conv_b_pw1': nrm((N_CONV_LAYERS, 2 * D_CONV), 0.02),
        'conv_w_dw': nrm((N_CONV_LAYERS, CONV_W, D_CONV), CONV_W ** -0.5),
        'conv_b_dw': nrm((N_CONV_LAYERS, D_CONV), 0.02),
        'conv_ln_g': 1.0 + nrm((N_CONV_LAYERS, D_CONV), 0.02),
        'conv_ln_b': nrm((N_CONV_LAYERS, D_CONV), 0.02),
        'conv_w_pw2': nrm((N_CONV_LAYERS, D_CONV, D), D_CONV ** -0.5),
        'conv_b_pw2': nrm((N_CONV_LAYERS, D), 0.02),
        'xattn_norm': 1.0 + nrm((DEPTH, D), 0.02),
        'xattn_w_q': nrm((DEPTH, D, D), D ** -0.5),
        'xattn_w_kv': nrm((DEPTH, D, 2 * D), D ** -0.5),
        'xattn_w_o': nrm((DEPTH, D, D), D ** -0.5),
        'final_norm': 1.0 + nrm((D,), 0.02),
    }


def reference(x_prompt, x_sample, cache_k_diff, cache_v_diff, state_conv, cache_mem_k, cache_mem_v,
              page_table, mem_prompt, ffn_norm, ffn_w_in, ffn_w_out, mix_norm, diff_w_qkv,
              diff_lambda, diff_subln, diff_w_o, rel_bias, conv_w_pw1, conv_b_pw1, conv_w_dw,
              conv_b_dw, conv_ln_g, conv_ln_b, conv_w_pw2, conv_b_pw2, xattn_norm, xattn_w_q,
              xattn_w_kv, xattn_w_o, final_norm):
    xp, xs = x_prompt, x_sample
    k_p, v_p, conv_p, mk_p, mv_p, k_s, v_s, conv_s = [], [], [], [], [], [], [], []
    q_pos_s = PAST_LEN + jnp.arange(DEC_SEQ)
    k_pos_s = jnp.arange(PAST_LEN + DEC_SEQ)
    for i in range(DEPTH):
        xp = xp + 0.5 * swiglu_ffn(xp, ffn_norm[i, 0], ffn_w_in[i, 0], ffn_w_out[i, 0])
        xs = xs + 0.5 * swiglu_ffn(xs, ffn_norm[i, 0], ffn_w_in[i, 0], ffn_w_out[i, 0])
        hp = rmsnorm(xp, mix_norm[i])
        hs = rmsnorm(xs, mix_norm[i])
        if i % N_MIXERS == 0:
            a = i // N_MIXERS
            lam, lam_init = compute_lambda(diff_lambda[a], i)
            q, k, v = diff_qkv(hp, diff_w_qkv[a])
            op = diff_attn_prompt(q, k, v, rel_bias, lam)
            k_p.append(k)
            v_p.append(v)
            xp = xp + diff_out(op, lam_init, diff_subln[a], diff_w_o[a])
            qs, ks, vs = diff_qkv(hs, diff_w_qkv[a])
            k_all = jnp.concatenate([gather_pages(cache_k_diff[a], page_table).astype(ks.dtype), ks], axis=1)
            v_all = jnp.concatenate([gather_pages(cache_v_diff[a], page_table).astype(vs.dtype), vs], axis=1)
            os_ = diff_attend(qs, k_all, v_all, q_pos_s, k_pos_s, rel_bias, lam)
            k_s.append(ks)
            v_s.append(vs)
            xs = xs + diff_out(os_, lam_init, diff_subln[a], diff_w_o[a])
        else:
            c = i // N_MIXERS
            cp = (conv_w_pw1[c], conv_b_pw1[c], conv_w_dw[c], conv_b_dw[c], conv_ln_g[c],
                  conv_ln_b[c], conv_w_pw2[c], conv_b_pw2[c])
            hist_p = jnp.zeros((xp.shape[0], CONV_W - 1, D_CONV), xp.dtype)
            yp_c, st_p = conv_module(hp, hist_p, *cp)
            ys_c, st_s = conv_module(hs, state_conv[c], *cp)
            conv_p.append(st_p)
            conv_s.append(st_s)
            xp = xp + yp_c
            xs = xs + ys_c
        mkp, mvp = mem_kv(mem_prompt, xattn_w_kv[i])
        mk_p.append(mkp)
        mv_p.append(mvp)
        xp = xp + cross_attend(rmsnorm(xp, xattn_norm[i]), mkp, mvp, xattn_w_q[i], xattn_w_o[i])
        xs = xs + cross_attend(rmsnorm(xs, xattn_norm[i]), cache_mem_k[i].astype(xs.dtype),
                               cache_mem_v[i].astype(xs.dtype), xattn_w_q[i], xattn_w_o[i])
        xp = xp + 0.5 * swiglu_ffn(xp, ffn_norm[i, 1], ffn_w_in[i, 1], ffn_w_out[i, 1])
        xs = xs + 0.5 * swiglu_ffn(xs, ffn_norm[i, 1], ffn_w_in[i, 1], ffn_w_out[i, 1])
    y_prompt = rmsnorm(xp, final_norm)
    y_sample = rmsnorm(xs, final_norm)
    return (y_prompt, y_sample, jnp.stack(k_p), jnp.stack(v_p), jnp.stack(conv_p),
            jnp.stack(mk_p), jnp.stack(mv_p), jnp.stack(k_s), jnp.stack(v_s), jnp.stack(conv_s))
```

```python
import functools
import math

import jax
import jax.numpy as jnp
from jax import lax
from jax.experimental import pallas as pl
from jax.experimental.pallas import tpu as pltpu

D_MODEL = 1024
BATCH = 8
SEQ = 2048
DEPTH = 4
DEC_BATCH = 32
DEC_SEQ = 8
PAST_LEN = 8192
PAGE_SIZE = 128
N_MIXERS = 2
N_HEADS = 8
DQK = D_MODEL // N_HEADS // 2
DV = 2 * DQK
N_BUCKETS = 32
MAX_DISTANCE = 128
CONV_W = 31
D_FF = 2816
N_MEM = 256
N_XHEADS = 4
XHEAD_DIM = D_MODEL // N_XHEADS
EPS = 1e-6

F32 = jnp.float32
BF16 = jnp.bfloat16

VMEM_LIMIT = 56 * 1024 * 1024
LANES = 128
SUBLANES = 8
HIST_PAD = 32
FFN_TF = 256
ROW_TILE = 1024
QKV_TM = 512
ATT_TQ = 256
ATT_TK = 256
PAGES_PER_STEP = 8
N_PAGES = PAST_LEN // PAGE_SIZE
QROWS = 2 * N_HEADS * DEC_SEQ


def _params(sem):
    return pltpu.CompilerParams(dimension_semantics=sem, vmem_limit_bytes=VMEM_LIMIT)


def _rms(x, g):
    return x * lax.rsqrt(jnp.mean(x * x, axis=-1, keepdims=True) + EPS) * g


def _dot(a, b):
    return jnp.dot(a, b, preferred_element_type=F32)


def _dot_nt(a, b):
    return lax.dot_general(a, b, (((1,), (1,)), ((), ())), preferred_element_type=F32)


def _ffn_kernel(x_ref, g_ref, wa_ref, wu_ref, wout_ref, o_ref, h_ref, acc_ref, *, nf):
    f = pl.program_id(1)

    @pl.when(f == 0)
    def _():
        h_ref[...] = _rms(x_ref[...], g_ref[...]).astype(BF16)
        acc_ref[...] = jnp.zeros_like(acc_ref)

    h = h_ref[...]
    a = _dot(h, wa_ref[...])
    act = (a * jax.nn.sigmoid(a)) * _dot(h, wu_ref[...])
    acc_ref[...] += _dot(act.astype(BF16), wout_ref[...])

    @pl.when(f == nf - 1)
    def _():
        o_ref[...] = x_ref[...] + 0.5 * acc_ref[...]


def _ffn(x, g, w_in, w_out, layer, half, tm):
    m = x.shape[0]
    nf = D_FF // FFN_TF
    return pl.pallas_call(
        functools.partial(_ffn_kernel, nf=nf),
        grid=(m // tm, nf),
        in_specs=[
            pl.BlockSpec((tm, D_MODEL), lambda i, f: (i, 0)),
            pl.BlockSpec((1, D_MODEL), lambda i, f: (0, 0)),
            pl.BlockSpec((None, None, D_MODEL, FFN_TF), lambda i, f: (layer, half, 0, f)),
            pl.BlockSpec((None, None, D_MODEL, FFN_TF), lambda i, f: (layer, half, 0, nf + f)),
            pl.BlockSpec((None, None, FFN_TF, D_MODEL), lambda i, f: (layer, half, f, 0)),
        ],
        out_specs=pl.BlockSpec((tm, D_MODEL), lambda i, f: (i, 0)),
        out_shape=jax.ShapeDtypeStruct((m, D_MODEL), F32),
        scratch_shapes=[pltpu.VMEM((tm, D_MODEL), BF16), pltpu.VMEM((tm, D_MODEL), F32)],
        compiler_params=_params(("parallel", "arbitrary")),
        name="ffn",
    )(x, g, w_in, w_in, w_out)


def _qkv_p_kernel(x_ref, g_ref, wq_ref, wkt_ref, wv_ref, q_ref, kt_ref, ktb_ref, v_ref, vb_ref):
    h = _rms(x_ref[...], g_ref[...]).astype(BF16)
    q_ref[...] = (_dot(h, wq_ref[...]) * (DQK ** -0.5)).astype(BF16)
    kt = _dot_nt(wkt_ref[...], h)
    kt_ref[...] = kt
    for c in range(kt.shape[1] // ATT_TK):
        ktb_ref[c] = kt[:, c * ATT_TK:(c + 1) * ATT_TK].astype(BF16)
    v = _dot(h, wv_ref[...])
    vb_ref[...] = v.astype(BF16)
    rows = v.shape[0]
    for n in range(N_HEADS):
        v_ref[pl.ds(n, rows, stride=N_HEADS), :] = v[:, n * DV:(n + 1) * DV]


def _qkv_prompt(x, g, wqkv, wkt, a, tm):
    nt = SEQ // tm
    ck = tm // ATT_TK
    row = pl.BlockSpec((tm, D_MODEL), lambda b, t: (b * nt + t, 0))
    return pl.pallas_call(
        _qkv_p_kernel,
        grid=(BATCH, nt),
        in_specs=[row,
                  pl.BlockSpec((1, D_MODEL), lambda b, t: (0, 0)),
                  pl.BlockSpec((None, D_MODEL, D_MODEL), lambda b, t: (a, 0, 0)),
                  pl.BlockSpec((None, D_MODEL, D_MODEL), lambda b, t: (a, 0, 0)),
                  pl.BlockSpec((None, D_MODEL, D_MODEL), lambda b, t: (a, 0, 2))],
        out_specs=[row,
                   pl.BlockSpec((None, D_MODEL, tm), lambda b, t: (b, 0, t)),
                   pl.BlockSpec((None, ck, D_MODEL, ATT_TK), lambda b, t: (b, t, 0, 0)),
                   pl.BlockSpec((tm * N_HEADS, DV), lambda b, t: (b * nt + t, 0)),
                   row],
        out_shape=[jax.ShapeDtypeStruct((BATCH * SEQ, D_MODEL), BF16),
                   jax.ShapeDtypeStruct((BATCH, D_MODEL, SEQ), F32),
                   jax.ShapeDtypeStruct((BATCH, SEQ // ATT_TK, D_MODEL, ATT_TK), BF16),
                   jax.ShapeDtypeStruct((BATCH * SEQ * N_HEADS, DV), F32),
                   jax.ShapeDtypeStruct((BATCH * SEQ, D_MODEL), BF16)],
        compiler_params=_params(("parallel", "parallel")),
        name="qkv_prompt",
    )(x, g, wqkv, wkt, wqkv)


def _qkv_s_kernel(x_ref, g_ref, w_ref, q_ref, k_ref, v_ref, h_ref):
    n = pl.program_id(0)

    @pl.when(n == 0)
    def _():
        h_ref[...] = _rms(x_ref[...], g_ref[...]).astype(BF16)

    y = _dot(h_ref[...], w_ref[...])

    @pl.when(n == 0)
    def _():
        q_ref[...] = y * (DQK ** -0.5)

    @pl.when(n == 1)
    def _():
        k_ref[...] = y

    @pl.when(n == 2)
    def _():
        v_ref[...] = y


def _qkv_sample(x, g, wqkv, a):
    m = x.shape[0]
    full = pl.BlockSpec((m, D_MODEL), lambda n: (0, 0))
    out = jax.ShapeDtypeStruct((m, D_MODEL), F32)
    return pl.pallas_call(
        _qkv_s_kernel,
        grid=(3,),
        in_specs=[full, pl.BlockSpec((1, D_MODEL), lambda n: (0, 0)),
                  pl.BlockSpec((None, D_MODEL, D_MODEL), lambda n: (a, 0, n))],
        out_specs=[full, full, full],
        out_shape=[out, out, out],
        scratch_shapes=[pltpu.VMEM((m, D_MODEL), BF16)],
        compiler_params=_params(("arbitrary",)),
        name="qkv_sample",
    )(x, g, wqkv)


def _bias_kernel(rb_ref, bucket_ref, o_ref):
    n = pl.program_id(0)
    bucket = bucket_ref[...]
    acc = jnp.zeros(bucket.shape, F32)
    for b in range(N_BUCKETS):
        acc = jnp.where(bucket == b, rb_ref[b, n], acc)
    o_ref[0] = jnp.where(bucket < 0, -jnp.inf, acc - rb_ref[N_BUCKETS - 1, n])


def _bias_tiles(rel_bias, bucket):
    r, c = bucket.shape
    return pl.pallas_call(
        _bias_kernel,
        grid=(N_HEADS,),
        in_specs=[pl.BlockSpec(memory_space=pltpu.SMEM),
                  pl.BlockSpec((r, c), lambda n: (0, 0))],
        out_specs=pl.BlockSpec((1, r, c), lambda n: (n, 0, 0)),
        out_shape=jax.ShapeDtypeStruct((N_HEADS, r, c), F32),
        compiler_params=_params(("arbitrary",)),
        name="bias_tiles",
    )(rel_bias, bucket)


def _rel_bucket(n):
    max_exact = N_BUCKETS // 2
    nf = jnp.maximum(n, 1).astype(F32)
    large = max_exact + (jnp.log(nf / max_exact) / math.log(MAX_DISTANCE / max_exact)
                         * (N_BUCKETS - max_exact)).astype(jnp.int32)
    large = jnp.minimum(large, N_BUCKETS - 1)
    return jnp.where(n < max_exact, n, large)


def _lambda(lp_ref, lam_init):
    lp = lp_ref[...]
    s01 = jnp.sum(lp[0:1] * lp[1:2], axis=-1, keepdims=True)
    s23 = jnp.sum(lp[2:3] * lp[3:4], axis=-1, keepdims=True)
    return jnp.exp(s01) - jnp.exp(s23) + lam_init


def _subln(o, g_ref, lam_init):
    return _rms(o, g_ref[...]) * (1.0 - lam_init)


def _attn_p_kernel(q_ref, k_ref, v_ref, bias_ref, lp_ref, g_ref, o_ref,
                   s_ref, m_ref, l_ref, acc_ref, *, lam_init):
    i = pl.program_id(2)
    tq, tk = ATT_TQ, ATT_TK
    q = q_ref[...]
    lane = lax.broadcasted_iota(jnp.int32, q.shape, 1)
    zero = jnp.zeros_like(q)
    qq = jnp.concatenate([jnp.where(lane < DQK, q, zero), jnp.where(lane >= DQK, q, zero)], axis=0)

    def score(j):
        return _dot(qq, k_ref[j])

    def lane_fold(s, op):
        r = s[:, :LANES]
        for c in range(1, tk // LANES):
            r = op(r, s[:, c * LANES:(c + 1) * LANES])
        return r

    m_ref[...] = jnp.full(m_ref.shape, -jnp.inf, F32)

    def put(j, s):
        s_ref[j] = s
        m_ref[...] = jnp.maximum(m_ref[...], lane_fold(s, jnp.maximum))

    def far_body(j, c):
        put(j, score(j))
        return c

    lax.fori_loop(0, jnp.maximum(i - 1, 0), far_body, 0)

    @pl.when(i >= 1)
    def _():
        b = bias_ref[0, 1]
        put(i - 1, score(i - 1) + jnp.concatenate([b, b], axis=0))

    b = bias_ref[0, 0]
    put(i, score(i) + jnp.concatenate([b, b], axis=0))

    m = jnp.max(m_ref[...], axis=-1, keepdims=True)
    l_ref[...] = jnp.zeros_like(l_ref)
    acc_ref[...] = jnp.zeros_like(acc_ref)

    def pv_body(j, c):
        p = jnp.exp(s_ref[j] - m)
        l_ref[...] += lane_fold(p, jnp.add)
        vt = v_ref[pl.ds(pl.multiple_of(j * tk, tk), tk), :]
        acc_ref[...] += _dot(p.astype(BF16), vt)
        return c

    lax.fori_loop(0, i + 1, pv_body, 0)

    l = jnp.sum(l_ref[...], axis=-1, keepdims=True)
    o = acc_ref[...] / l
    lam = _lambda(lp_ref, lam_init)
    o = o[:tq] - lam * o[tq:]
    o_ref[...] = _subln(o, g_ref, lam_init).astype(o_ref.dtype)


def _attn_prompt(q, ktb, vb, bias, lp, g, lam_init):
    nq = SEQ // ATT_TQ
    nk = SEQ // ATT_TK
    return pl.pallas_call(
        functools.partial(_attn_p_kernel, lam_init=lam_init),
        grid=(BATCH, N_HEADS, nq),
        in_specs=[
            pl.BlockSpec((ATT_TQ, DV), lambda b, n, i: (b * nq + i, n)),
            pl.BlockSpec((None, nk, DV, ATT_TK), lambda b, n, i: (b, 0, n, 0)),
            pl.BlockSpec((SEQ, DV), lambda b, n, i: (b, n)),
            pl.BlockSpec((1, 2, ATT_TQ, ATT_TK), lambda b, n, i: (n, 0, 0, 0)),
            pl.BlockSpec((4, DQK), lambda b, n, i: (0, 0)),
            pl.BlockSpec((1, DV), lambda b, n, i: (0, 0)),
        ],
        out_specs=pl.BlockSpec((ATT_TQ, DV), lambda b, n, i: (b * nq + i, n)),
        out_shape=jax.ShapeDtypeStruct((BATCH * SEQ, D_MODEL), BF16),
        scratch_shapes=[pltpu.VMEM((nk, 2 * ATT_TQ, ATT_TK), F32),
                        pltpu.VMEM((2 * ATT_TQ, LANES), F32),
                        pltpu.VMEM((2 * ATT_TQ, LANES), F32),
                        pltpu.VMEM((2 * ATT_TQ, DV), F32)],
        compiler_params=_params(("parallel", "parallel", "arbitrary")),
        name="attn_prompt",
    )(q, ktb, vb, bias, lp, g)


def _attn_s_kernel(pt_ref, q_ref, *refs, lam_init):
    del pt_ref
    npg = PAGES_PER_STEP
    k_refs = refs[:npg]
    v_refs = refs[npg:2 * npg]
    (knew_ref, vnew_ref, bias_ref, lp_ref, g_ref, o_ref,
     qbd_ref, m_ref, l_ref, acc_ref) = refs[2 * npg:]
    g = pl.program_id(1)
    ng = pl.num_programs(1)
    hrows = 2 * DEC_SEQ

    @pl.when(g == 0)
    def _():
        q = q_ref[...]
        qt = jnp.broadcast_to(q[None], (2 * N_HEADS, DEC_SEQ, D_MODEL)).reshape(QROWS, D_MODEL)
        r = lax.broadcasted_iota(jnp.int32, (QROWS, D_MODEL), 0)
        c = lax.broadcasted_iota(jnp.int32, (QROWS, D_MODEL), 1)
        qbd_ref[...] = jnp.where((c >> 6) == (r >> 3), qt, 0.0).astype(BF16)
        m_ref[...] = jnp.full(m_ref.shape, -jnp.inf, F32)
        l_ref[...] = jnp.zeros_like(l_ref)
        acc_ref[...] = jnp.zeros_like(acc_ref)

    qbd = qbd_ref[...]

    def update(s, v_heads):
        m_old = m_ref[...]
        m_new = jnp.maximum(m_old, jnp.max(s, axis=-1, keepdims=True))
        alpha = jnp.exp(m_old - m_new)
        p = jnp.exp(s - m_new)
        l_ref[...] = alpha * l_ref[...] + jnp.sum(p, axis=-1, keepdims=True)
        pb = p.astype(BF16)
        pv = jnp.concatenate([_dot(pb[n * hrows:(n + 1) * hrows], v_heads[n]) for n in range(N_HEADS)],
                             axis=0)
        acc_ref[...] = alpha * acc_ref[...] + pv
        m_ref[...] = m_new

    ktb = jnp.concatenate([r[0, 0].astype(BF16) for r in k_refs], axis=1)
    s = _dot(qbd, ktb)
    last = jnp.where(g == ng - 1, bias_ref[:, :PAGE_SIZE], 0.0)
    s = jnp.concatenate([s[:, :(npg - 1) * PAGE_SIZE], s[:, (npg - 1) * PAGE_SIZE:] + last], axis=1)
    v_heads = [jnp.concatenate([r[0, 0, pl.ds(n, PAGE_SIZE, stride=N_HEADS), :].astype(BF16)
                                for r in v_refs], axis=0) for n in range(N_HEADS)]
    update(s, v_heads)

    @pl.when(g == ng - 1)
    def _():
        pad = jnp.zeros((PAGE_SIZE - DEC_SEQ, D_MODEL), F32)
        kn = jnp.concatenate([knew_ref[...], pad], axis=0).astype(BF16)
        vn = jnp.concatenate([vnew_ref[...], pad], axis=0).astype(BF16)
        update(_dot_nt(qbd, kn) + bias_ref[:, PAGE_SIZE:],
               [vn[:, n * DV:(n + 1) * DV] for n in range(N_HEADS)])
        o_all = acc_ref[...] / l_ref[...]
        lam = _lambda(lp_ref, lam_init)
        for n in range(N_HEADS):
            o0 = o_all[n * hrows:n * hrows + DEC_SEQ]
            o1 = o_all[n * hrows + DEC_SEQ:(n + 1) * hrows]
            o_ref[:, n * DV:(n + 1) * DV] = _subln(o0 - lam * o1, g_ref, lam_init)


def _attn_sample(page_table, q, cache_kt, cache_v, layer, knew, vnew, bias, lp, g, lam_init):
    npg = PAGES_PER_STEP
    ng = N_PAGES // npg

    def page_spec(i, rows):
        return pl.BlockSpec((1, 1, rows, LANES),
                            lambda b, gg, pt: (layer, pt[b * N_PAGES + gg * npg + i], 0, 0))

    row8 = pl.BlockSpec((DEC_SEQ, D_MODEL), lambda b, gg, pt: (b, 0))
    grid_spec = pltpu.PrefetchScalarGridSpec(
        num_scalar_prefetch=1,
        grid=(DEC_BATCH, ng),
        in_specs=([row8] + [page_spec(i, D_MODEL) for i in range(npg)]
                  + [page_spec(i, PAGE_SIZE * N_HEADS) for i in range(npg)] + [
            row8, row8,
            pl.BlockSpec((QROWS, 2 * PAGE_SIZE), lambda b, gg, pt: (0, 0)),
            pl.BlockSpec((4, DQK), lambda b, gg, pt: (0, 0)),
            pl.BlockSpec((1, DV), lambda b, gg, pt: (0, 0)),
        ]),
        out_specs=row8,
        scratch_shapes=[pltpu.VMEM((QROWS, D_MODEL), BF16),
                        pltpu.VMEM((QROWS, 1), F32),
                        pltpu.VMEM((QROWS, 1), F32),
                        pltpu.VMEM((QROWS, DV), F32)],
    )
    return pl.pallas_call(
        functools.partial(_attn_s_kernel, lam_init=lam_init),
        grid_spec=grid_spec,
        out_shape=jax.ShapeDtypeStruct((DEC_BATCH * DEC_SEQ, D_MODEL), F32),
        compiler_params=_params(("parallel", "arbitrary")),
        name="attn_sample",
    )(page_table, q, *([cache_kt] * npg), *([cache_v] * npg), knew, vnew, bias, lp, g)


def _proj_res_kernel(a_ref, w_ref, x_ref, o_ref):
    o_ref[...] = x_ref[...] + _dot(a_ref[...].astype(BF16), w_ref[...])


def _proj_res(a, w, layer, x, tm):
    m = x.shape[0]
    row = pl.BlockSpec((tm, D_MODEL), lambda i: (i, 0))
    return pl.pallas_call(
        _proj_res_kernel,
        grid=(m // tm,),
        in_specs=[row, pl.BlockSpec((None, D_MODEL, D_MODEL), lambda i: (layer, 0, 0)), row],
        out_specs=row,
        out_shape=jax.ShapeDtypeStruct((m, D_MODEL), F32),
        compiler_params=_params(("parallel",)),
        name="proj_res",
    )(a, w, x)


def _conv_kernel(x_ref, hist_ref, g_ref, w1_ref, b1_ref, wdw_ref, bdw_ref, lng_ref, lnb_ref,
                 w2_ref, b2_ref, o_ref, st_ref, buf_ref, *, tt):
    t = pl.program_id(1)

    @pl.when(t == 0)
    def _():
        buf_ref[0:HIST_PAD, :] = hist_ref[0]

    @pl.when(t > 0)
    def _():
        buf_ref[0:HIST_PAD, :] = buf_ref[tt:tt + HIST_PAD, :]

    x = x_ref[...]
    h = _rms(x, g_ref[...]).astype(BF16)
    ag = _dot(h, w1_ref[...]) + b1_ref[...]
    buf_ref[HIST_PAD:, :] = ag[:, :D_MODEL] * jax.nn.sigmoid(ag[:, D_MODEL:])

    off = HIST_PAD - (CONV_W - 1)
    y = jnp.zeros((tt, D_MODEL), F32) + bdw_ref[...]
    for w in range(CONV_W):
        y = y + buf_ref[off + w:off + w + tt, :] * wdw_ref[w:w + 1, :]
    mu = jnp.mean(y, axis=-1, keepdims=True)
    yc = y - mu
    z = yc * lax.rsqrt(jnp.mean(yc * yc, axis=-1, keepdims=True) + EPS) * lng_ref[...] + lnb_ref[...]
    z = z * jax.nn.sigmoid(z)
    o_ref[...] = x + _dot(z.astype(BF16), w2_ref[...]) + b2_ref[...]
    st_ref[0] = buf_ref[tt:tt + HIST_PAD, :]


def _conv_module(x, hist, g, w1, b1, wdw, bdw, lng, lnb, w2, b2, cidx, nb, t_len, tt):
    nt = t_len // tt
    row = pl.BlockSpec((tt, D_MODEL), lambda b, t: (b * nt + t, 0))
    vec = pl.BlockSpec((1, D_MODEL), lambda b, t: (0, 0))
    st = pl.BlockSpec((1, HIST_PAD, D_MODEL), lambda b, t: (b, 0, 0))
    return pl.pallas_call(
        functools.partial(_conv_kernel, tt=tt),
        grid=(nb, nt),
        in_specs=[row, st, vec,
                  pl.BlockSpec((None, D_MODEL, 2 * D_MODEL), lambda b, t: (cidx, 0, 0)),
                  pl.BlockSpec((1, 2 * D_MODEL), lambda b, t: (0, 0)),
                  pl.BlockSpec((HIST_PAD, D_MODEL), lambda b, t: (0, 0)),
                  vec, vec, vec,
                  pl.BlockSpec((None, D_MODEL, D_MODEL), lambda b, t: (cidx, 0, 0)),
                  vec],
        out_specs=[row, st],
        out_shape=[jax.ShapeDtypeStruct((nb * t_len, D_MODEL), F32),
                   jax.ShapeDtypeStruct((nb, HIST_PAD, D_MODEL), F32)],
        scratch_shapes=[pltpu.VMEM((HIST_PAD + tt, D_MODEL), F32)],
        compiler_params=_params(("parallel", "arbitrary")),
        name="conv_module",
    )(x, hist, g, w1, b1, wdw, bdw, lng, lnb, w2, b2)


def _memkv_kernel(m_ref, w_ref, o_ref, ob_ref):
    y = _dot(m_ref[...].astype(BF16), w_ref[...])
    o_ref[...] = y
    ob_ref[...] = y.astype(BF16)


def _mem_kv(mem, w_kv):
    m = mem.shape[0]
    out = pl.BlockSpec((None, None, m, D_MODEL), lambda i, j: (i, j, 0, 0))
    return pl.pallas_call(
        _memkv_kernel,
        grid=(DEPTH, 2),
        in_specs=[pl.BlockSpec((m, D_MODEL), lambda i, j: (0, 0)),
                  pl.BlockSpec((None, D_MODEL, D_MODEL), lambda i, j: (i, 0, j))],
        out_specs=[out, out],
        out_shape=[jax.ShapeDtypeStruct((DEPTH, 2, m, D_MODEL), F32),
                   jax.ShapeDtypeStruct((DEPTH, 2, m, D_MODEL), BF16)],
        compiler_params=_params(("parallel", "parallel")),
        name="mem_kv",
    )(mem, w_kv)


def _xhead(q, mk, mv):
    s = _dot_nt(q, mk)
    p = jnp.exp(s - jnp.max(s, axis=-1, keepdims=True))
    l = jnp.sum(p, axis=-1, keepdims=True)
    return _dot(p.astype(BF16), mv) / l


def _xattn_p_kernel(x_ref, g_ref, wq_ref, mk_ref, mv_ref, wo_ref, o_ref):
    x = x_ref[...]
    h = _rms(x, g_ref[...]).astype(BF16)
    q = (_dot(h, wq_ref[...]) * (XHEAD_DIM ** -0.5)).astype(BF16)
    outs = []
    for hd in range(N_XHEADS):
        cs = slice(hd * XHEAD_DIM, (hd + 1) * XHEAD_DIM)
        outs.append(_xhead(q[:, cs], mk_ref[:, cs], mv_ref[:, cs]))
    o = jnp.concatenate(outs, axis=-1)
    o_ref[...] = x + _dot(o.astype(BF16), wo_ref[...])


def _xattn_prompt(x, g, wq, mkv_b, layer, wo, tm):
    m = x.shape[0]
    per_b = SEQ // tm
    row = pl.BlockSpec((tm, D_MODEL), lambda i: (i, 0))
    wspec = pl.BlockSpec((None, D_MODEL, D_MODEL), lambda i: (layer, 0, 0))

    def mem_spec(j):
        return pl.BlockSpec((None, None, N_MEM, D_MODEL), lambda i: (layer, j, i // per_b, 0))

    return pl.pallas_call(
        _xattn_p_kernel,
        grid=(m // tm,),
        in_specs=[row, pl.BlockSpec((1, D_MODEL), lambda i: (0, 0)), wspec,
                  mem_spec(0), mem_spec(1), wspec],
        out_specs=row,
        out_shape=jax.ShapeDtypeStruct((m, D_MODEL), F32),
        compiler_params=_params(("parallel",)),
        name="xattn_prompt",
    )(x, g, wq, mkv_b, mkv_b, wo)


def _xattn_s_kernel(x_ref, g_ref, wq_ref, mk_ref, mv_ref, wo_ref, o_ref, q_ref, a_ref):
    b = pl.program_id(0)

    @pl.when(b == 0)
    def _():
        h = _rms(x_ref[...], g_ref[...]).astype(BF16)
        q_ref[...] = _dot(h, wq_ref[...]) * (XHEAD_DIM ** -0.5)

    r0 = pl.multiple_of(b * DEC_SEQ, DEC_SEQ)
    q = q_ref[pl.ds(r0, DEC_SEQ), :].astype(BF16)
    outs = []
    for hd in range(N_XHEADS):
        outs.append(_xhead(q[:, hd * XHEAD_DIM:(hd + 1) * XHEAD_DIM],
                           mk_ref[:, hd, :].astype(BF16), mv_ref[:, hd, :].astype(BF16)))
    a_ref[pl.ds(r0, DEC_SEQ), :] = jnp.concatenate(outs, axis=-1)

    @pl.when(b == pl.num_programs(0) - 1)
    def _():
        o_ref[...] = x_ref[...] + _dot(a_ref[...].astype(BF16), wo_ref[...])


def _xattn_sample(x, g, wq, mem_k, mem_v, layer, wo):
    m = x.shape[0]
    full = pl.BlockSpec((m, D_MODEL), lambda b: (0, 0))
    wspec = pl.BlockSpec((None, D_MODEL, D_MODEL), lambda b: (layer, 0, 0))
    mem = pl.BlockSpec((None, None, N_MEM, N_XHEADS, XHEAD_DIM), lambda b: (layer, b, 0, 0, 0))
    return pl.pallas_call(
        _xattn_s_kernel,
        grid=(DEC_BATCH,),
        in_specs=[full, pl.BlockSpec((1, D_MODEL), lambda b: (0, 0)), wspec, mem, mem, wspec],
        out_specs=full,
        out_shape=jax.ShapeDtypeStruct((m, D_MODEL), F32),
        scratch_shapes=[pltpu.VMEM((m, D_MODEL), F32), pltpu.VMEM((m, D_MODEL), F32)],
        compiler_params=_params(("arbitrary",)),
        name="xattn_sample",
    )(x, g, wq, mem_k, mem_v, wo)


def _norm_kernel(x_ref, g_ref, o_ref):
    o_ref[...] = _rms(x_ref[...], g_ref[...])


def _final_norm(x, g, tm):
    m = x.shape[0]
    row = pl.BlockSpec((tm, D_MODEL), lambda i: (i, 0))
    return pl.pallas_call(
        _norm_kernel,
        grid=(m // tm,),
        in_specs=[row, pl.BlockSpec((1, D_MODEL), lambda i: (0, 0))],
        out_specs=row,
        out_shape=jax.ShapeDtypeStruct((m, D_MODEL), F32),
        compiler_params=_params(("parallel",)),
        name="final_norm",
    )(x, g)


def _vec(v):
    return v.reshape(1, -1).astype(F32)


def kernel(x_prompt, x_sample, cache_k_diff, cache_v_diff, state_conv, cache_mem_k, cache_mem_v, page_table, mem_prompt, ffn_norm, ffn_w_in, ffn_w_out, mix_norm, diff_w_qkv, diff_lambda, diff_subln, diff_w_o, rel_bias, conv_w_pw1, conv_b_pw1, conv_w_dw, conv_b_dw, conv_ln_g, conv_ln_b, conv_w_pw2, conv_b_pw2, xattn_norm, xattn_w_q, xattn_w_kv, xattn_w_o, final_norm):
    mp = BATCH * SEQ
    ms = DEC_BATCH * DEC_SEQ
    xp = x_prompt.reshape(mp, D_MODEL)
    xs = x_sample.reshape(ms, D_MODEL)
    tm_p, tm_s = ROW_TILE, ms

    w_in = ffn_w_in.astype(BF16)
    w_out = ffn_w_out.astype(BF16)
    wqkv = diff_w_qkv.astype(BF16)
    wkt = jnp.swapaxes(diff_w_qkv[:, :, D_MODEL:2 * D_MODEL], 1, 2).astype(BF16)
    w_do = diff_w_o.astype(BF16)
    w_pw1 = conv_w_pw1.astype(BF16)
    w_pw2 = conv_w_pw2.astype(BF16)
    w_xq = xattn_w_q.astype(BF16)
    w_xkv = xattn_w_kv.astype(BF16)
    w_xo = xattn_w_o.astype(BF16)

    r = jnp.arange(ATT_TQ)[:, None]
    c = jnp.arange(ATT_TK)[None, :]
    d0 = r - c
    bucket_p = jnp.concatenate([jnp.where(d0 >= 0, _rel_bucket(jnp.maximum(d0, 0)), -1),
                                _rel_bucket(d0 + ATT_TK)], axis=0).astype(jnp.int32)
    bias_p = _bias_tiles(rel_bias, bucket_p).reshape(N_HEADS, 2, ATT_TQ, ATT_TK)
    rs = jnp.arange(DEC_SEQ)[:, None]
    cs = jnp.arange(PAGE_SIZE)[None, :]
    d_new = rs - cs
    bucket_s = jnp.concatenate(
        [_rel_bucket(PAGE_SIZE + rs - cs),
         jnp.where((d_new >= 0) & (cs < DEC_SEQ), _rel_bucket(jnp.maximum(d_new, 0)), -1)],
        axis=1).astype(jnp.int32)
    bias_s = _bias_tiles(rel_bias, bucket_s)
    bias_s = jnp.broadcast_to(bias_s[:, None], (N_HEADS, 2, DEC_SEQ, 2 * PAGE_SIZE)).reshape(
        QROWS, 2 * PAGE_SIZE)

    n_pool = cache_k_diff.shape[1]
    ckt = cache_k_diff.transpose(0, 1, 3, 4, 2).reshape(-1, n_pool, D_MODEL, PAGE_SIZE)
    cv = cache_v_diff.reshape(-1, n_pool, PAGE_SIZE * N_HEADS, DV)
    pt = page_table.reshape(-1).astype(jnp.int32)

    mkv, mkv_b = _mem_kv(mem_prompt.reshape(BATCH * N_MEM, D_MODEL), w_xkv)

    k_p, v_p, conv_p, k_s, v_s, conv_s = [], [], [], [], [], []
    for i in range(DEPTH):
        g = _vec(ffn_norm[i, 0])
        xp = _ffn(xp, g, w_in, w_out, i, 0, tm_p)
        xs = _ffn(xs, g, w_in, w_out, i, 0, tm_s)
        gm = _vec(mix_norm[i])
        if i % N_MIXERS == 0:
            a = i // N_MIXERS
            lam_init = 0.8 - 0.6 * math.exp(-0.3 * i)
            lp = diff_lambda[a].astype(F32)
            gs = _vec(diff_subln[a])
            q, kt, ktb, v, vb = _qkv_prompt(xp, gm, wqkv, wkt, a, QKV_TM)
            k_p.append(kt)
            v_p.append(v)
            op = _attn_prompt(q, ktb, vb, bias_p, lp, gs, lam_init)
            xp = _proj_res(op, w_do, a, xp, tm_p)
            q, k, v = _qkv_sample(xs, gm, wqkv, a)
            k_s.append(k)
            v_s.append(v)
            os_ = _attn_sample(pt, q, ckt, cv, a, k, v, bias_s, lp, gs, lam_init)
            xs = _proj_res(os_, w_do, a, xs, tm_s)
        else:
            cidx = i // N_MIXERS
            cw = (w_pw1, _vec(conv_b_pw1[cidx]),
                  jnp.pad(conv_w_dw[cidx], ((0, HIST_PAD - CONV_W), (0, 0))), _vec(conv_b_dw[cidx]),
                  _vec(conv_ln_g[cidx]), _vec(conv_ln_b[cidx]),
                  w_pw2, _vec(conv_b_pw2[cidx]))
            pad = HIST_PAD - (CONV_W - 1)
            hist_p = jnp.zeros((BATCH, HIST_PAD, D_MODEL), F32)
            hist_s = jnp.pad(state_conv[cidx], ((0, 0), (pad, 0), (0, 0)))
            xp, st_p = _conv_module(xp, hist_p, gm, *cw, cidx, BATCH, SEQ, 512)
            xs, st_s = _conv_module(xs, hist_s, gm, *cw, cidx, DEC_BATCH, DEC_SEQ, DEC_SEQ)
            conv_p.append(st_p[:, pad:])
            conv_s.append(st_s[:, pad:])
        gx = _vec(xattn_norm[i])
        xp = _xattn_prompt(xp, gx, w_xq, mkv_b, i, w_xo, 512)
        xs = _xattn_sample(xs, gx, w_xq, cache_mem_k, cache_mem_v, i, w_xo)
        g = _vec(ffn_norm[i, 1])
        xp = _ffn(xp, g, w_in, w_out, i, 1, tm_p)
        xs = _ffn(xs, g, w_in, w_out, i, 1, tm_s)

    gf = _vec(final_norm)
    y_prompt = _final_norm(xp, gf, tm_p).reshape(BATCH, SEQ, D_MODEL)
    y_sample = _final_norm(xs, gf, tm_s).reshape(DEC_BATCH, DEC_SEQ, D_MODEL)
    na = len(k_p)
    mem_shape = (DEPTH, BATCH, N_MEM, N_XHEADS, XHEAD_DIM)
    new_k_prompt = jnp.stack(k_p).reshape(na, BATCH, 2 * N_HEADS, DQK, SEQ).transpose(0, 1, 4, 2, 3)
    return (y_prompt, y_sample,
            new_k_prompt,
            jnp.stack(v_p).reshape(na, BATCH, SEQ, N_HEADS, DV),
            jnp.stack(conv_p),
            mkv[:, 0].reshape(mem_shape),
            mkv[:, 1].reshape(mem_shape),
            jnp.stack(k_s).reshape(na, DEC_BATCH, DEC_SEQ, 2 * N_HEADS, DQK),
            jnp.stack(v_s).reshape(na, DEC_BATCH, DEC_SEQ, N_HEADS, DV),
            jnp.stack(conv_s))
```

```python
import functools
import math

import jax
import jax.numpy as jnp
from jax import lax
from jax.experimental import pallas as pl
from jax.experimental.pallas import tpu as pltpu

D_MODEL = 1024
BATCH = 8
SEQ = 2048
DEPTH = 4
DEC_BATCH = 32
DEC_SEQ = 8
PAST_LEN = 8192
PAGE_SIZE = 128
N_MIXERS = 2
N_HEADS = 8
DQK = D_MODEL // N_HEADS // 2
DV = 2 * DQK
N_BUCKETS = 32
MAX_DISTANCE = 128
CONV_W = 31
D_FF = 2816
N_MEM = 256
N_XHEADS = 4
XHEAD_DIM = D_MODEL // N_XHEADS
EPS = 1e-6
LOG2E = math.log2(math.e)

F32 = jnp.float32
BF16 = jnp.bfloat16

VMEM_LIMIT = 56 * 1024 * 1024
LANES = 128
SUBLANES = 8
HIST_PAD = 32
FFN_CHUNKS = ((0, 1536), (1536, D_FF))
ROW_TILE = 1024
FFN_TM = 512
QKV_TM = 512
ATT_TQ = 256
ATT_TK = 256
PAGES_PER_STEP = 8
N_PAGES = PAST_LEN // PAGE_SIZE
QROWS = 2 * N_HEADS * DEC_SEQ


def _params(sem):
    return pltpu.CompilerParams(dimension_semantics=sem, vmem_limit_bytes=VMEM_LIMIT)


def _rms(x, g):
    return x * lax.rsqrt(jnp.mean(x * x, axis=-1, keepdims=True) + EPS) * g


def _dot(a, b):
    return jnp.dot(a, b, preferred_element_type=F32)


def _dot_nt(a, b):
    return lax.dot_general(a, b, (((1,), (1,)), ((), ())), preferred_element_type=F32)


def _ffn_kernel(x_ref, g_ref, win_ref, wout_ref, *rest, final):
    x = x_ref[...]
    h = _rms(x, g_ref[...]).astype(BF16)
    acc = None
    for c0, c1 in FFN_CHUNKS:
        a = _dot(h, win_ref[:, c0:c1])
        u = _dot(h, win_ref[:, D_FF + c0:D_FF + c1])
        act = ((a * jax.nn.sigmoid(a)) * u).astype(BF16)
        part = _dot(act, wout_ref[c0:c1, :])
        acc = part if acc is None else acc + part
    y = x + 0.5 * acc
    if final:
        gf_ref, o_ref = rest
        o_ref[...] = _rms(y, gf_ref[...])
    else:
        rest[0][...] = y


def _ffn(x, g, w_in, w_out, layer, half, tm, final_g=None):
    m = x.shape[0]
    row = pl.BlockSpec((tm, D_MODEL), lambda i: (i, 0))
    vec = pl.BlockSpec((1, D_MODEL), lambda i: (0, 0))
    resident = dict(pipeline_mode=pl.Buffered(1))
    in_specs = [row, vec,
                pl.BlockSpec((None, None, D_MODEL, 2 * D_FF), lambda i: (layer, half, 0, 0), **resident),
                pl.BlockSpec((None, None, D_FF, D_MODEL), lambda i: (layer, half, 0, 0), **resident)]
    args = [x, g, w_in, w_out]
    if final_g is not None:
        in_specs.append(vec)
        args.append(final_g)
    return pl.pallas_call(
        functools.partial(_ffn_kernel, final=final_g is not None),
        grid=(m // tm,),
        in_specs=in_specs,
        out_specs=row,
        out_shape=jax.ShapeDtypeStruct((m, D_MODEL), F32),
        compiler_params=_params(("parallel",)),
        name="ffn",
    )(*args)


def _qkv_p_kernel(x_ref, g_ref, wqt_ref, wkt_ref, wvt_ref, wk_ref, wv_ref, *rest):
    qt_ref, kt_ref, kb_ref, v_ref, vt_ref = rest[-5:]
    h = _rms(x_ref[...], g_ref[...]).astype(BF16)
    qt_ref[...] = (_dot_nt(wqt_ref[...], h) * (DQK ** -0.5 * LOG2E)).astype(BF16)
    kt_ref[...] = _dot_nt(wkt_ref[...], h)
    vt_ref[...] = _dot_nt(wvt_ref[...], h).astype(BF16)
    kb_ref[...] = _dot(h, wk_ref[...]).astype(BF16)
    v = _dot(h, wv_ref[...])
    rows = v.shape[0]
    for n in range(N_HEADS):
        v_ref[pl.ds(n, rows, stride=N_HEADS), :] = v[:, n * DV:(n + 1) * DV]


def _qkv_prompt(x, g, wqkv, wqkv_t, a, n_layers, tm, prev=None):
    nt = SEQ // tm
    row = pl.BlockSpec((tm, D_MODEL), lambda b, t: (b * nt + t, 0))
    tr = pl.BlockSpec((None, D_MODEL, tm), lambda b, t: (b, 0, t))

    def wspec(j):
        return pl.BlockSpec((None, D_MODEL, D_MODEL), lambda b, t: (a, 0, j), pipeline_mode=pl.Buffered(1))

    def wtspec(j):
        return pl.BlockSpec((None, D_MODEL, D_MODEL), lambda b, t: (a, j, 0), pipeline_mode=pl.Buffered(1))

    in_specs = [row, pl.BlockSpec((1, D_MODEL), lambda b, t: (0, 0)),
                wtspec(0), wtspec(1), wtspec(2), wspec(1), wspec(2)]
    args = [x, g, wqkv_t, wqkv_t, wqkv_t, wqkv, wqkv]
    aliases = {}
    if prev is not None:
        in_specs += [pl.BlockSpec(memory_space=pl.ANY)] * 2
        aliases = {len(args): 1, len(args) + 1: 3}
        args += list(prev)
    tr_sds = jax.ShapeDtypeStruct((BATCH, D_MODEL, SEQ), BF16)
    return pl.pallas_call(
        _qkv_p_kernel,
        grid=(BATCH, nt),
        in_specs=in_specs,
        out_specs=[tr,
                   pl.BlockSpec((None, None, D_MODEL, tm), lambda b, t: (a, b, 0, t)),
                   row,
                   pl.BlockSpec((None, tm * N_HEADS, DV), lambda b, t: (a, b * nt + t, 0)),
                   tr],
        out_shape=[tr_sds,
                   jax.ShapeDtypeStruct((n_layers, BATCH, D_MODEL, SEQ), F32),
                   jax.ShapeDtypeStruct((BATCH * SEQ, D_MODEL), BF16),
                   jax.ShapeDtypeStruct((n_layers, BATCH * SEQ * N_HEADS, DV), F32),
                   tr_sds],
        input_output_aliases=aliases,
        compiler_params=_params(("parallel", "parallel")),
        name="qkv_prompt",
    )(*args)


def _qkv_s_kernel(x_ref, g_ref, w_ref, q_ref, k_ref, v_ref, h_ref):
    n = pl.program_id(0)

    @pl.when(n == 0)
    def _():
        h_ref[...] = _rms(x_ref[...], g_ref[...]).astype(BF16)

    y = _dot(h_ref[...], w_ref[...])

    @pl.when(n == 0)
    def _():
        q_ref[...] = y * (DQK ** -0.5)

    @pl.when(n == 1)
    def _():
        k_ref[...] = y

    @pl.when(n == 2)
    def _():
        v_ref[...] = y


def _qkv_sample(x, g, wqkv, a):
    m = x.shape[0]
    full = pl.BlockSpec((m, D_MODEL), lambda n: (0, 0))
    out = jax.ShapeDtypeStruct((m, D_MODEL), F32)
    return pl.pallas_call(
        _qkv_s_kernel,
        grid=(3,),
        in_specs=[full, pl.BlockSpec((1, D_MODEL), lambda n: (0, 0)),
                  pl.BlockSpec((None, D_MODEL, D_MODEL), lambda n: (a, 0, n))],
        out_specs=[full, full, full],
        out_shape=[out, out, out],
        scratch_shapes=[pltpu.VMEM((m, D_MODEL), BF16)],
        compiler_params=_params(("arbitrary",)),
        name="qkv_sample",
    )(x, g, wqkv)


def _bias_kernel(rb_ref, bucket_ref, o_ref, *, scale):
    n = pl.program_id(0)
    bucket = bucket_ref[...]
    acc = jnp.zeros(bucket.shape, F32)
    for b in range(N_BUCKETS):
        acc = jnp.where(bucket == b, rb_ref[b, n], acc)
    o_ref[0] = jnp.where(bucket < 0, -jnp.inf, (acc - rb_ref[N_BUCKETS - 1, n]) * scale)


def _bias_tiles(rel_bias, bucket, scale=1.0):
    r, c = bucket.shape
    return pl.pallas_call(
        functools.partial(_bias_kernel, scale=scale),
        grid=(N_HEADS,),
        in_specs=[pl.BlockSpec(memory_space=pltpu.SMEM),
                  pl.BlockSpec((r, c), lambda n: (0, 0))],
        out_specs=pl.BlockSpec((1, r, c), lambda n: (n, 0, 0)),
        out_shape=jax.ShapeDtypeStruct((N_HEADS, r, c), F32),
        compiler_params=_params(("arbitrary",)),
        name="bias_tiles",
    )(rel_bias, bucket)


def _rel_bucket(n):
    max_exact = N_BUCKETS // 2
    nf = jnp.maximum(n, 1).astype(F32)
    large = max_exact + (jnp.log(nf / max_exact) / math.log(MAX_DISTANCE / max_exact)
                         * (N_BUCKETS - max_exact)).astype(jnp.int32)
    large = jnp.minimum(large, N_BUCKETS - 1)
    return jnp.where(n < max_exact, n, large)


def _lambda(lp_ref, lam_init):
    lp = lp_ref[...]
    s01 = jnp.sum(lp[0:1] * lp[1:2], axis=-1, keepdims=True)
    s23 = jnp.sum(lp[2:3] * lp[3:4], axis=-1, keepdims=True)
    return jnp.exp(s01) - jnp.exp(s23) + lam_init


def _subln(o, g_ref, lam_init):
    return _rms(o, g_ref[...]) * (1.0 - lam_init)


def _fold_rows(s, op):
    acc = s[0:SUBLANES]
    for r in range(1, s.shape[0] // SUBLANES):
        acc = op(acc, s[r * SUBLANES:(r + 1) * SUBLANES])
    return acc


def _attn_p_scores(c, qt_ref, k_ref, bias_ref, s_ref):
    tq, tk = ATT_TQ, ATT_TK
    qt = qt_ref[:, c * tq:(c + 1) * tq]
    row = lax.broadcasted_iota(jnp.int32, qt.shape, 0)
    zero = jnp.zeros_like(qt)
    qq = jnp.concatenate([jnp.where(row < DQK, qt, zero), jnp.where(row >= DQK, qt, zero)], axis=1)
    m8 = None
    for j in range(c + 1):
        s = _dot(k_ref[j * tk:(j + 1) * tk, :], qq)
        if c - j < 2:
            s = s + bias_ref[0, c - j]
        s_ref[j] = s
        t = _fold_rows(s, jnp.maximum)
        m8 = t if m8 is None else jnp.maximum(m8, t)
    return jnp.max(m8, axis=0, keepdims=True)


def _attn_p_softmax(c, m, s_ref, p_ref):
    tk = ATT_TK
    l8 = None
    for j in range(c + 1):
        p = jnp.exp2(s_ref[j] - m)
        t = _fold_rows(p, jnp.add)
        l8 = t if l8 is None else l8 + t
        p_ref[j * tk:(j + 1) * tk, :] = p.astype(BF16)
    return jnp.sum(l8, axis=0, keepdims=True)


def _attn_p_out(c, l, lam, vt_ref, g_ref, o_ref, p_ref, lam_init):
    tq = ATT_TQ
    kend = (c + 1) * ATT_TK
    ot = _dot(vt_ref[:, :kend], p_ref[:kend, :]) / l
    o = (ot[:, :tq] - lam * ot[:, tq:]).T
    o_ref[c * tq:(c + 1) * tq, :] = _subln(o, g_ref, lam_init).astype(o_ref.dtype)


def _attn_p_kernel(qt_ref, k_ref, vt_ref, bias_ref, lp_ref, g_ref, o_ref, s_ref, p_ref, *, lam_init):
    lam = _lambda(lp_ref, lam_init)
    nq = SEQ // ATT_TQ
    m = _attn_p_scores(0, qt_ref, k_ref, bias_ref, s_ref.at[0])
    for c in range(nq):
        m_next = None
        if c + 1 < nq:
            m_next = _attn_p_scores(c + 1, qt_ref, k_ref, bias_ref, s_ref.at[(c + 1) % 2])
        l = _attn_p_softmax(c, m, s_ref.at[c % 2], p_ref.at[c % 2])
        _attn_p_out(c, l, lam, vt_ref, g_ref, o_ref, p_ref.at[c % 2], lam_init)
        m = m_next


def _attn_prompt(qt, kb, vt, bias, lp, g, lam_init):
    nk = SEQ // ATT_TK
    return pl.pallas_call(
        functools.partial(_attn_p_kernel, lam_init=lam_init),
        grid=(BATCH, N_HEADS),
        in_specs=[
            pl.BlockSpec((None, DV, SEQ), lambda b, n: (b, n, 0)),
            pl.BlockSpec((SEQ, DV), lambda b, n: (b, n)),
            pl.BlockSpec((None, DV, SEQ), lambda b, n: (b, n, 0)),
            pl.BlockSpec((1, 2, ATT_TK, 2 * ATT_TQ), lambda b, n: (n, 0, 0, 0)),
            pl.BlockSpec((4, DQK), lambda b, n: (0, 0)),
            pl.BlockSpec((1, DV), lambda b, n: (0, 0)),
        ],
        out_specs=pl.BlockSpec((SEQ, DV), lambda b, n: (b, n)),
        out_shape=jax.ShapeDtypeStruct((BATCH * SEQ, D_MODEL), BF16),
        scratch_shapes=[pltpu.VMEM((2, nk, ATT_TK, 2 * ATT_TQ), F32),
                        pltpu.VMEM((2, SEQ, 2 * ATT_TQ), BF16)],
        compiler_params=_params(("parallel", "parallel")),
        name="attn_prompt",
    )(qt, kb, vt, bias, lp, g)


def _attn_s_kernel(pt_ref, q_ref, *refs, lam_init):
    del pt_ref
    npg = PAGES_PER_STEP
    k_refs = refs[:npg]
    v_refs = refs[npg:2 * npg]
    (knew_ref, vnew_ref, bias_ref, lp_ref, g_ref, o_ref,
     qbd_ref, m_ref, l_ref, acc_ref) = refs[2 * npg:]
    g = pl.program_id(1)
    ng = pl.num_programs(1)
    hrows = 2 * DEC_SEQ

    @pl.when(g == 0)
    def _():
        q = q_ref[...]
        qt = jnp.broadcast_to(q[None], (2 * N_HEADS, DEC_SEQ, D_MODEL)).reshape(QROWS, D_MODEL)
        r = lax.broadcasted_iota(jnp.int32, (QROWS, D_MODEL), 0)
        c = lax.broadcasted_iota(jnp.int32, (QROWS, D_MODEL), 1)
        qbd_ref[...] = jnp.where((c >> 6) == (r >> 3), qt, 0.0).astype(BF16)
        m_ref[...] = jnp.full(m_ref.shape, -jnp.inf, F32)
        l_ref[...] = jnp.zeros_like(l_ref)
        acc_ref[...] = jnp.zeros_like(acc_ref)

    qbd = qbd_ref[...]

    def update(s, v_heads):
        m_old = m_ref[...]
        m_new = jnp.maximum(m_old, jnp.max(s, axis=-1, keepdims=True))
        alpha = jnp.exp(m_old - m_new)
        p = jnp.exp(s - m_new)
        l_ref[...] = alpha * l_ref[...] + jnp.sum(p, axis=-1, keepdims=True)
        pb = p.astype(BF16)
        pv = jnp.concatenate([_dot(pb[n * hrows:(n + 1) * hrows], v_heads[n]) for n in range(N_HEADS)],
                             axis=0)
        acc_ref[...] = alpha * acc_ref[...] + pv
        m_ref[...] = m_new

    ktb = jnp.concatenate([r[0, 0].astype(BF16) for r in k_refs], axis=1)
    s = _dot(qbd, ktb)
    last = jnp.where(g == ng - 1, bias_ref[:, :PAGE_SIZE], 0.0)
    s = jnp.concatenate([s[:, :(npg - 1) * PAGE_SIZE], s[:, (npg - 1) * PAGE_SIZE:] + last], axis=1)
    v_heads = [jnp.concatenate([r[0, 0, pl.ds(n, PAGE_SIZE, stride=N_HEADS), :].astype(BF16)
                                for r in v_refs], axis=0) for n in range(N_HEADS)]
    update(s, v_heads)

    @pl.when(g == ng - 1)
    def _():
        pad = jnp.zeros((PAGE_SIZE - DEC_SEQ, D_MODEL), F32)
        kn = jnp.concatenate([knew_ref[...], pad], axis=0).astype(BF16)
        vn = jnp.concatenate([vnew_ref[...], pad], axis=0).astype(BF16)
        update(_dot_nt(qbd, kn) + bias_ref[:, PAGE_SIZE:],
               [vn[:, n * DV:(n + 1) * DV] for n in range(N_HEADS)])
        o_all = acc_ref[...] / l_ref[...]
        lam = _lambda(lp_ref, lam_init)
        for n in range(N_HEADS):
            o0 = o_all[n * hrows:n * hrows + DEC_SEQ]
            o1 = o_all[n * hrows + DEC_SEQ:(n + 1) * hrows]
            o_ref[:, n * DV:(n + 1) * DV] = _subln(o0 - lam * o1, g_ref, lam_init)


def _attn_sample(page_table, q, cache_kt, cache_v, layer, knew, vnew, bias, lp, g, lam_init):
    npg = PAGES_PER_STEP
    ng = N_PAGES // npg

    def page_spec(i, rows):
        return pl.BlockSpec((1, 1, rows, LANES),
                            lambda b, gg, pt: (layer, pt[b * N_PAGES + gg * npg + i], 0, 0))

    row8 = pl.BlockSpec((DEC_SEQ, D_MODEL), lambda b, gg, pt: (b, 0))
    grid_spec = pltpu.PrefetchScalarGridSpec(
        num_scalar_prefetch=1,
        grid=(DEC_BATCH, ng),
        in_specs=([row8] + [page_spec(i, D_MODEL) for i in range(npg)]
                  + [page_spec(i, PAGE_SIZE * N_HEADS) for i in range(npg)] + [
            row8, row8,
            pl.BlockSpec((QROWS, 2 * PAGE_SIZE), lambda b, gg, pt: (0, 0)),
            pl.BlockSpec((4, DQK), lambda b, gg, pt: (0, 0)),
            pl.BlockSpec((1, DV), lambda b, gg, pt: (0, 0)),
        ]),
        out_specs=row8,
        scratch_shapes=[pltpu.VMEM((QROWS, D_MODEL), BF16),
                        pltpu.VMEM((QROWS, 1), F32),
                        pltpu.VMEM((QROWS, 1), F32),
                        pltpu.VMEM((QROWS, DV), F32)],
    )
    return pl.pallas_call(
        functools.partial(_attn_s_kernel, lam_init=lam_init),
        grid_spec=grid_spec,
        out_shape=jax.ShapeDtypeStruct((DEC_BATCH * DEC_SEQ, D_MODEL), F32),
        compiler_params=_params(("parallel", "arbitrary")),
        name="attn_sample",
    )(page_table, q, *([cache_kt] * npg), *([cache_v] * npg), knew, vnew, bias, lp, g)


def _proj_res_kernel(a_ref, w_ref, x_ref, o_ref):
    o_ref[...] = x_ref[...] + _dot(a_ref[...].astype(BF16), w_ref[...])


def _proj_res(a, w, layer, x, tm):
    m = x.shape[0]
    row = pl.BlockSpec((tm, D_MODEL), lambda i: (i, 0))
    return pl.pallas_call(
        _proj_res_kernel,
        grid=(m // tm,),
        in_specs=[row, pl.BlockSpec((None, D_MODEL, D_MODEL), lambda i: (layer, 0, 0)), row],
        out_specs=row,
        out_shape=jax.ShapeDtypeStruct((m, D_MODEL), F32),
        compiler_params=_params(("parallel",)),
        name="proj_res",
    )(a, w, x)


def _conv_kernel(x_ref, hist_ref, g_ref, w1_ref, b1_ref, wdw_ref, bdw_ref, lng_ref, lnb_ref,
                 w2_ref, b2_ref, o_ref, st_ref, buf_ref, *win_refs, nbb, tt):
    t = pl.program_id(1)

    @pl.when(t == 0)
    def _():
        buf_ref[:, 0:HIST_PAD, :] = hist_ref[...]

    @pl.when(t > 0)
    def _():
        buf_ref[:, 0:HIST_PAD, :] = buf_ref[:, tt:tt + HIST_PAD, :]

    x = x_ref[...]
    h = _rms(x, g_ref[...]).astype(BF16)
    ag = _dot(h, w1_ref[...]) + b1_ref[...]
    u = ag[:, :D_MODEL] * jax.nn.sigmoid(ag[:, D_MODEL:])
    buf_ref[:, HIST_PAD:, :] = u.reshape(nbb, tt, D_MODEL)

    off = HIST_PAD - (CONV_W - 1)
    y = None
    for b in range(SUBLANES):
        es = [e for e in range(off, off + CONV_W) if e % SUBLANES == b]
        if b == 0:
            win_ref = buf_ref
        else:
            win_ref = win_refs[b % 2]
            win_ref[...] = buf_ref[:, b:b + win_ref.shape[1], :]
        for e in es:
            term = win_ref[:, e - b:e - b + tt, :] * wdw_ref[e - off:e - off + 1, :]
            y = term if y is None else y + term
    y = y.reshape(nbb * tt, D_MODEL) + bdw_ref[...]
    mu = jnp.mean(y, axis=-1, keepdims=True)
    yc = y - mu
    z = yc * lax.rsqrt(jnp.mean(yc * yc, axis=-1, keepdims=True) + EPS) * lng_ref[...] + lnb_ref[...]
    z = z * jax.nn.sigmoid(z)
    o_ref[...] = x + _dot(z.astype(BF16), w2_ref[...]) + b2_ref[...]
    st_ref[...] = buf_ref[:, tt:tt + HIST_PAD, :]


def _conv_module(x, hist, g, w1, b1, wdw, bdw, lng, lnb, w2, b2, cidx, nb, nbb, t_len, tt):
    nt = t_len // tt
    row = pl.BlockSpec((nbb * tt, D_MODEL), lambda b, t: (b * nt + t, 0))
    vec = pl.BlockSpec((1, D_MODEL), lambda b, t: (0, 0))
    st = pl.BlockSpec((nbb, HIST_PAD, D_MODEL), lambda b, t: (b, 0, 0))
    return pl.pallas_call(
        functools.partial(_conv_kernel, nbb=nbb, tt=tt),
        grid=(nb // nbb, nt),
        in_specs=[row, st, vec,
                  pl.BlockSpec((None, D_MODEL, 2 * D_MODEL), lambda b, t: (cidx, 0, 0)),
                  pl.BlockSpec((1, 2 * D_MODEL), lambda b, t: (0, 0)),
                  pl.BlockSpec((HIST_PAD, D_MODEL), lambda b, t: (0, 0)),
                  vec, vec, vec,
                  pl.BlockSpec((None, D_MODEL, D_MODEL), lambda b, t: (cidx, 0, 0)),
                  vec],
        out_specs=[row, st],
        out_shape=[jax.ShapeDtypeStruct((nb * t_len, D_MODEL), F32),
                   jax.ShapeDtypeStruct((nb, HIST_PAD, D_MODEL), F32)],
        scratch_shapes=[pltpu.VMEM((nbb, HIST_PAD + tt, D_MODEL), F32)]
                       + [pltpu.VMEM((nbb, HIST_PAD - SUBLANES + tt, D_MODEL), F32)] * 2,
        compiler_params=_params(("parallel", "arbitrary")),
        name="conv_module",
    )(x, hist, g, w1, b1, wdw, bdw, lng, lnb, w2, b2)


def _memkv_kernel(m_ref, w_ref, o_ref, ob_ref):
    y = _dot(m_ref[...].astype(BF16), w_ref[...])
    o_ref[...] = y
    ob_ref[...] = y.astype(BF16)


def _mem_kv(mem, w_kv):
    m = mem.shape[0]
    out = pl.BlockSpec((None, None, m, D_MODEL), lambda i, j: (i, j, 0, 0))
    return pl.pallas_call(
        _memkv_kernel,
        grid=(DEPTH, 2),
        in_specs=[pl.BlockSpec((m, D_MODEL), lambda i, j: (0, 0)),
                  pl.BlockSpec((None, D_MODEL, D_MODEL), lambda i, j: (i, 0, j))],
        out_specs=[out, out],
        out_shape=[jax.ShapeDtypeStruct((DEPTH, 2, m, D_MODEL), F32),
                   jax.ShapeDtypeStruct((DEPTH, 2, m, D_MODEL), BF16)],
        compiler_params=_params(("parallel", "parallel")),
        name="mem_kv",
    )(mem, w_kv)


def _xhead(q, mk, mv):
    s = _dot_nt(q, mk)
    p = jnp.exp(s - jnp.max(s, axis=-1, keepdims=True))
    l = jnp.sum(p, axis=-1, keepdims=True)
    return _dot(p.astype(BF16), mv) / l


def _xattn_p_kernel(x_ref, g_ref, wq_ref, mk_ref, mv_ref, wo_ref, o_ref):
    x = x_ref[...]
    h = _rms(x, g_ref[...]).astype(BF16)
    q = (_dot(h, wq_ref[...]) * (XHEAD_DIM ** -0.5)).astype(BF16)
    outs = []
    for hd in range(N_XHEADS):
        cs = slice(hd * XHEAD_DIM, (hd + 1) * XHEAD_DIM)
        outs.append(_xhead(q[:, cs], mk_ref[:, cs], mv_ref[:, cs]))
    o = jnp.concatenate(outs, axis=-1)
    o_ref[...] = x + _dot(o.astype(BF16), wo_ref[...])


def _xattn_prompt(x, g, wq, mkv_b, layer, wo, tm):
    m = x.shape[0]
    per_b = SEQ // tm
    row = pl.BlockSpec((tm, D_MODEL), lambda i: (i, 0))
    wspec = pl.BlockSpec((None, D_MODEL, D_MODEL), lambda i: (layer, 0, 0))

    def mem_spec(j):
        return pl.BlockSpec((None, None, N_MEM, D_MODEL), lambda i: (layer, j, i // per_b, 0))

    return pl.pallas_call(
        _xattn_p_kernel,
        grid=(m // tm,),
        in_specs=[row, pl.BlockSpec((1, D_MODEL), lambda i: (0, 0)), wspec,
                  mem_spec(0), mem_spec(1), wspec],
        out_specs=row,
        out_shape=jax.ShapeDtypeStruct((m, D_MODEL), F32),
        compiler_params=_params(("parallel",)),
        name="xattn_prompt",
    )(x, g, wq, mkv_b, mkv_b, wo)


def _xattn_s_kernel(x_ref, g_ref, wq_ref, mk_ref, mv_ref, wo_ref, o_ref, q_ref, a_ref):
    b = pl.program_id(0)

    @pl.when(b == 0)
    def _():
        h = _rms(x_ref[...], g_ref[...]).astype(BF16)
        q_ref[...] = _dot(h, wq_ref[...]) * (XHEAD_DIM ** -0.5)

    r0 = pl.multiple_of(b * DEC_SEQ, DEC_SEQ)
    q = q_ref[pl.ds(r0, DEC_SEQ), :].astype(BF16)
    outs = []
    for hd in range(N_XHEADS):
        cs = slice(hd * XHEAD_DIM, (hd + 1) * XHEAD_DIM)
        outs.append(_xhead(q[:, cs], mk_ref[:, cs], mv_ref[:, cs]))
    a_ref[pl.ds(r0, DEC_SEQ), :] = jnp.concatenate(outs, axis=-1)

    @pl.when(b == pl.num_programs(0) - 1)
    def _():
        o_ref[...] = x_ref[...] + _dot(a_ref[...].astype(BF16), wo_ref[...])


def _xattn_sample(x, g, wq, mem_k, mem_v, layer, wo):
    m = x.shape[0]
    full = pl.BlockSpec((m, D_MODEL), lambda b: (0, 0))
    wspec = pl.BlockSpec((None, D_MODEL, D_MODEL), lambda b: (layer, 0, 0))
    mem = pl.BlockSpec((None, None, N_MEM, D_MODEL), lambda b: (layer, b, 0, 0))
    return pl.pallas_call(
        _xattn_s_kernel,
        grid=(DEC_BATCH,),
        in_specs=[full, pl.BlockSpec((1, D_MODEL), lambda b: (0, 0)), wspec, mem, mem, wspec],
        out_specs=full,
        out_shape=jax.ShapeDtypeStruct((m, D_MODEL), F32),
        scratch_shapes=[pltpu.VMEM((m, D_MODEL), F32), pltpu.VMEM((m, D_MODEL), F32)],
        compiler_params=_params(("arbitrary",)),
        name="xattn_sample",
    )(x, g, wq, mem_k, mem_v, wo)


def _vec(v):
    return v.reshape(1, -1).astype(F32)


def kernel(x_prompt, x_sample, cache_k_diff, cache_v_diff, state_conv, cache_mem_k, cache_mem_v, page_table, mem_prompt, ffn_norm, ffn_w_in, ffn_w_out, mix_norm, diff_w_qkv, diff_lambda, diff_subln, diff_w_o, rel_bias, conv_w_pw1, conv_b_pw1, conv_w_dw, conv_b_dw, conv_ln_g, conv_ln_b, conv_w_pw2, conv_b_pw2, xattn_norm, xattn_w_q, xattn_w_kv, xattn_w_o, final_norm):
    mp = BATCH * SEQ
    ms = DEC_BATCH * DEC_SEQ
    xp = x_prompt.reshape(mp, D_MODEL)
    xs = x_sample.reshape(ms, D_MODEL)
    tm_p, tm_s = ROW_TILE, ms

    w_in = ffn_w_in.astype(BF16)
    w_out = ffn_w_out.astype(BF16)
    wqkv = diff_w_qkv.astype(BF16)
    wqkv_t = jnp.swapaxes(diff_w_qkv, 1, 2).astype(BF16)
    w_do = diff_w_o.astype(BF16)
    w_pw1 = conv_w_pw1.astype(BF16)
    w_pw2 = conv_w_pw2.astype(BF16)
    w_xq = xattn_w_q.astype(BF16)
    w_xkv = xattn_w_kv.astype(BF16)
    w_xo = xattn_w_o.astype(BF16)

    kk = jnp.arange(ATT_TK)[:, None]
    qq = jnp.arange(ATT_TQ)[None, :]
    d0 = qq - kk
    bucket_p = jnp.concatenate([jnp.where(d0 >= 0, _rel_bucket(jnp.maximum(d0, 0)), -1),
                                _rel_bucket(d0 + ATT_TK)], axis=0).astype(jnp.int32)
    bucket_p = jnp.concatenate([bucket_p, bucket_p], axis=1)
    bias_p = _bias_tiles(rel_bias, bucket_p, LOG2E).reshape(N_HEADS, 2, ATT_TK, 2 * ATT_TQ)
    rs = jnp.arange(DEC_SEQ)[:, None]
    cs = jnp.arange(PAGE_SIZE)[None, :]
    d_new = rs - cs
    bucket_s = jnp.concatenate(
        [_rel_bucket(PAGE_SIZE + rs - cs),
         jnp.where((d_new >= 0) & (cs < DEC_SEQ), _rel_bucket(jnp.maximum(d_new, 0)), -1)],
        axis=1).astype(jnp.int32)
    bias_s = _bias_tiles(rel_bias, bucket_s)
    bias_s = jnp.broadcast_to(bias_s[:, None], (N_HEADS, 2, DEC_SEQ, 2 * PAGE_SIZE)).reshape(
        QROWS, 2 * PAGE_SIZE)

    n_pool = cache_k_diff.shape[1]
    ckt = cache_k_diff.transpose(0, 1, 3, 4, 2).reshape(-1, n_pool, D_MODEL, PAGE_SIZE)
    cv = cache_v_diff.reshape(-1, n_pool, PAGE_SIZE * N_HEADS, DV)
    pt = page_table.reshape(-1).astype(jnp.int32)

    mkv, mkv_b = _mem_kv(mem_prompt.reshape(BATCH * N_MEM, D_MODEL), w_xkv)
    cmk = cache_mem_k.reshape(DEPTH, DEC_BATCH, N_MEM, D_MODEL).astype(BF16)
    cmv = cache_mem_v.reshape(DEPTH, DEC_BATCH, N_MEM, D_MODEL).astype(BF16)

    n_attn = (DEPTH + N_MIXERS - 1) // N_MIXERS
    gf = _vec(final_norm)
    kv_p = None
    conv_p, k_s, v_s, conv_s = [], [], [], []
    for i in range(DEPTH):
        g = _vec(ffn_norm[i, 0])
        xp = _ffn(xp, g, w_in, w_out, i, 0, FFN_TM)
        xs = _ffn(xs, g, w_in, w_out, i, 0, tm_s)
        gm = _vec(mix_norm[i])
        if i % N_MIXERS == 0:
            a = i // N_MIXERS
            lam_init = 0.8 - 0.6 * math.exp(-0.3 * i)
            lp = diff_lambda[a].astype(F32)
            gs = _vec(diff_subln[a])
            qt, kt_all, kb, v_all, vt = _qkv_prompt(xp, gm, wqkv, wqkv_t, a, n_attn, QKV_TM, prev=kv_p)
            kv_p = (kt_all, v_all)
            op = _attn_prompt(qt, kb, vt, bias_p, lp, gs, lam_init)
            xp = _proj_res(op, w_do, a, xp, tm_p)
            q, k, v = _qkv_sample(xs, gm, wqkv, a)
            k_s.append(k)
            v_s.append(v)
            os_ = _attn_sample(pt, q, ckt, cv, a, k, v, bias_s, lp, gs, lam_init)
            xs = _proj_res(os_, w_do, a, xs, tm_s)
        else:
            cidx = i // N_MIXERS
            cw = (w_pw1, _vec(conv_b_pw1[cidx]),
                  jnp.pad(conv_w_dw[cidx], ((0, HIST_PAD - CONV_W), (0, 0))), _vec(conv_b_dw[cidx]),
                  _vec(conv_ln_g[cidx]), _vec(conv_ln_b[cidx]),
                  w_pw2, _vec(conv_b_pw2[cidx]))
            pad = HIST_PAD - (CONV_W - 1)
            hist_p = jnp.zeros((BATCH, HIST_PAD, D_MODEL), F32)
            hist_s = jnp.pad(state_conv[cidx], ((0, 0), (pad, 0), (0, 0)))
            xp, st_p = _conv_module(xp, hist_p, gm, *cw, cidx, BATCH, 1, SEQ, 512)
            xs, st_s = _conv_module(xs, hist_s, gm, *cw, cidx, DEC_BATCH, DEC_BATCH, DEC_SEQ, DEC_SEQ)
            conv_p.append(st_p[:, pad:])
            conv_s.append(st_s[:, pad:])
        gx = _vec(xattn_norm[i])
        xp = _xattn_prompt(xp, gx, w_xq, mkv_b, i, w_xo, 512)
        xs = _xattn_sample(xs, gx, w_xq, cmk, cmv, i, w_xo)
        g = _vec(ffn_norm[i, 1])
        last = gf if i == DEPTH - 1 else None
        xp = _ffn(xp, g, w_in, w_out, i, 1, FFN_TM, final_g=last)
        xs = _ffn(xs, g, w_in, w_out, i, 1, tm_s, final_g=last)

    y_prompt = xp.reshape(BATCH, SEQ, D_MODEL)
    y_sample = xs.reshape(DEC_BATCH, DEC_SEQ, D_MODEL)
    na = n_attn
    kt_all, v_all = kv_p
    mem_shape = (DEPTH, BATCH, N_MEM, N_XHEADS, XHEAD_DIM)
    new_k_prompt = kt_all.reshape(na, BATCH, 2 * N_HEADS, DQK, SEQ).transpose(0, 1, 4, 2, 3)
    return (y_prompt, y_sample,
            new_k_prompt,
            v_all.reshape(na, BATCH, SEQ, N_HEADS, DV),
            jnp.stack(conv_p),
            mkv[:, 0].reshape(mem_shape),
            mkv[:, 1].reshape(mem_shape),
            jnp.stack(k_s).reshape(na, DEC_BATCH, DEC_SEQ, 2 * N_HEADS, DQK),
            jnp.stack(v_s).reshape(na, DEC_BATCH, DEC_SEQ, N_HEADS, DV),
            jnp.stack(conv_s))
```

```python
import functools
import math

import jax
import jax.numpy as jnp
from jax import lax
from jax.experimental import pallas as pl
from jax.experimental.pallas import tpu as pltpu

D_MODEL = 1024
BATCH = 8
SEQ = 2048
DEPTH = 4
DEC_BATCH = 32
DEC_SEQ = 8
PAST_LEN = 8192
PAGE_SIZE = 128
N_MIXERS = 2
N_HEADS = 8
DQK = D_MODEL // N_HEADS // 2
DV = 2 * DQK
N_BUCKETS = 32
MAX_DISTANCE = 128
CONV_W = 31
D_FF = 2816
N_MEM = 256
N_XHEADS = 4
XHEAD_DIM = D_MODEL // N_XHEADS
EPS = 1e-6
LOG2E = math.log2(math.e)

F32 = jnp.float32
BF16 = jnp.bfloat16

VMEM_LIMIT = 56 * 1024 * 1024
LANES = 128
SUBLANES = 8
HIST_PAD = 32
FFN_CHUNKS = ((0, 1536), (1536, D_FF))
ROW_TILE = 1024
FFN_TM = 512
QKV_TM = 512
ATT_TQ = 256
ATT_TK = 256
ONES_ROWS = 16
PAGES_PER_STEP = 16
HT_ROWS = N_XHEADS * (XHEAD_DIM // LANES)
N_PAGES = PAST_LEN // PAGE_SIZE
QROWS = 2 * N_HEADS * DEC_SEQ


def _params(sem):
    return pltpu.CompilerParams(dimension_semantics=sem, vmem_limit_bytes=VMEM_LIMIT)


def _rms(x, g):
    return x * lax.rsqrt(jnp.mean(x * x, axis=-1, keepdims=True) + EPS) * g


def _dot(a, b):
    return jnp.dot(a, b, preferred_element_type=F32)


def _dot_nt(a, b):
    return lax.dot_general(a, b, (((1,), (1,)), ((), ())), preferred_element_type=F32)


def _ffn_kernel(x_ref, g_ref, win_ref, wout_ref, *rest, final):
    x = x_ref[...]
    h = _rms(x, g_ref[...]).astype(BF16)
    acc = None
    for c0, c1 in FFN_CHUNKS:
        a = _dot(h, win_ref[:, c0:c1])
        u = _dot(h, win_ref[:, D_FF + c0:D_FF + c1])
        act = ((a * jax.nn.sigmoid(a)) * u).astype(BF16)
        part = _dot(act, wout_ref[c0:c1, :])
        acc = part if acc is None else acc + part
    y = x + 0.5 * acc
    if final:
        gf_ref, o_ref = rest
        o_ref[...] = _rms(y, gf_ref[...])
    else:
        rest[0][...] = y


def _ffn(x, g, w_in, w_out, layer, half, tm, final_g=None):
    m = x.shape[0]
    row = pl.BlockSpec((tm, D_MODEL), lambda i: (i, 0))
    vec = pl.BlockSpec((1, D_MODEL), lambda i: (0, 0))
    resident = dict(pipeline_mode=pl.Buffered(1))
    in_specs = [row, vec,
                pl.BlockSpec((None, None, D_MODEL, 2 * D_FF), lambda i: (layer, half, 0, 0), **resident),
                pl.BlockSpec((None, None, D_FF, D_MODEL), lambda i: (layer, half, 0, 0), **resident)]
    args = [x, g, w_in, w_out]
    if final_g is not None:
        in_specs.append(vec)
        args.append(final_g)
    return pl.pallas_call(
        functools.partial(_ffn_kernel, final=final_g is not None),
        grid=(m // tm,),
        in_specs=in_specs,
        out_specs=row,
        out_shape=jax.ShapeDtypeStruct((m, D_MODEL), F32),
        compiler_params=_params(("parallel",)),
        name="ffn",
    )(*args)


def _qkv_p_kernel(x_ref, g_ref, wqt_ref, wkt_ref, wvt_ref, wk_ref, wv_ref, *rest):
    qt_ref, kt_ref, kb_ref, v_ref, vt_ref = rest[-5:]
    h = _rms(x_ref[...], g_ref[...]).astype(BF16)
    qt_ref[...] = (_dot_nt(wqt_ref[...], h) * (DQK ** -0.5 * LOG2E)).astype(BF16)
    kt_ref[...] = _dot_nt(wkt_ref[...], h)
    vt_ref[...] = _dot_nt(wvt_ref[...], h).astype(BF16)
    kb_ref[...] = _dot(h, wk_ref[...]).astype(BF16)
    v = _dot(h, wv_ref[...])
    rows = v.shape[0]
    for n in range(N_HEADS):
        v_ref[pl.ds(n, rows, stride=N_HEADS), :] = v[:, n * DV:(n + 1) * DV]


def _qkv_prompt(x, g, wqkv, wqkv_t, a, n_layers, tm, prev=None):
    nt = SEQ // tm
    row = pl.BlockSpec((tm, D_MODEL), lambda b, t: (b * nt + t, 0))
    tr = pl.BlockSpec((None, D_MODEL, tm), lambda b, t: (b, 0, t))

    def wspec(j):
        return pl.BlockSpec((None, D_MODEL, D_MODEL), lambda b, t: (a, 0, j), pipeline_mode=pl.Buffered(1))

    def wtspec(j):
        return pl.BlockSpec((None, D_MODEL, D_MODEL), lambda b, t: (a, j, 0), pipeline_mode=pl.Buffered(1))

    in_specs = [row, pl.BlockSpec((1, D_MODEL), lambda b, t: (0, 0)),
                wtspec(0), wtspec(1), wtspec(2), wspec(1), wspec(2)]
    args = [x, g, wqkv_t, wqkv_t, wqkv_t, wqkv, wqkv]
    aliases = {}
    if prev is not None:
        in_specs += [pl.BlockSpec(memory_space=pl.ANY)] * 2
        aliases = {len(args): 1, len(args) + 1: 3}
        args += list(prev)
    tr_sds = jax.ShapeDtypeStruct((BATCH, D_MODEL, SEQ), BF16)
    return pl.pallas_call(
        _qkv_p_kernel,
        grid=(BATCH, nt),
        in_specs=in_specs,
        out_specs=[tr,
                   pl.BlockSpec((None, None, D_MODEL, tm), lambda b, t: (a, b, 0, t)),
                   row,
                   pl.BlockSpec((None, tm * N_HEADS, DV), lambda b, t: (a, b * nt + t, 0)),
                   tr],
        out_shape=[tr_sds,
                   jax.ShapeDtypeStruct((n_layers, BATCH, D_MODEL, SEQ), F32),
                   jax.ShapeDtypeStruct((BATCH * SEQ, D_MODEL), BF16),
                   jax.ShapeDtypeStruct((n_layers, BATCH * SEQ * N_HEADS, DV), F32),
                   tr_sds],
        input_output_aliases=aliases,
        compiler_params=_params(("parallel", "parallel")),
        name="qkv_prompt",
    )(*args)


def _qkv_s_kernel(x_ref, g_ref, w_ref, q_ref, k_ref, v_ref, h_ref):
    n = pl.program_id(0)

    @pl.when(n == 0)
    def _():
        h_ref[...] = _rms(x_ref[...], g_ref[...]).astype(BF16)

    y = _dot(h_ref[...], w_ref[...])

    @pl.when(n == 0)
    def _():
        q_ref[...] = y * (DQK ** -0.5)

    @pl.when(n == 1)
    def _():
        k_ref[...] = y

    @pl.when(n == 2)
    def _():
        v_ref[...] = y


def _qkv_sample(x, g, wqkv, a):
    m = x.shape[0]
    full = pl.BlockSpec((m, D_MODEL), lambda n: (0, 0))
    out = jax.ShapeDtypeStruct((m, D_MODEL), F32)
    return pl.pallas_call(
        _qkv_s_kernel,
        grid=(3,),
        in_specs=[full, pl.BlockSpec((1, D_MODEL), lambda n: (0, 0)),
                  pl.BlockSpec((None, D_MODEL, D_MODEL), lambda n: (a, 0, n))],
        out_specs=[full, full, full],
        out_shape=[out, out, out],
        scratch_shapes=[pltpu.VMEM((m, D_MODEL), BF16)],
        compiler_params=_params(("arbitrary",)),
        name="qkv_sample",
    )(x, g, wqkv)


def _bias_kernel(rb_ref, bucket_ref, o_ref, *, scale):
    n = pl.program_id(0)
    bucket = bucket_ref[...]
    acc = jnp.zeros(bucket.shape, F32)
    for b in range(N_BUCKETS):
        acc = jnp.where(bucket == b, rb_ref[b, n], acc)
    o_ref[0] = jnp.where(bucket < 0, -jnp.inf, (acc - rb_ref[N_BUCKETS - 1, n]) * scale)


def _bias_tiles(rel_bias, bucket, scale=1.0):
    r, c = bucket.shape
    return pl.pallas_call(
        functools.partial(_bias_kernel, scale=scale),
        grid=(N_HEADS,),
        in_specs=[pl.BlockSpec(memory_space=pltpu.SMEM),
                  pl.BlockSpec((r, c), lambda n: (0, 0))],
        out_specs=pl.BlockSpec((1, r, c), lambda n: (n, 0, 0)),
        out_shape=jax.ShapeDtypeStruct((N_HEADS, r, c), F32),
        compiler_params=_params(("arbitrary",)),
        name="bias_tiles",
    )(rel_bias, bucket)


def _rel_bucket(n):
    max_exact = N_BUCKETS // 2
    nf = jnp.maximum(n, 1).astype(F32)
    large = max_exact + (jnp.log(nf / max_exact) / math.log(MAX_DISTANCE / max_exact)
                         * (N_BUCKETS - max_exact)).astype(jnp.int32)
    large = jnp.minimum(large, N_BUCKETS - 1)
    return jnp.where(n < max_exact, n, large)


def _lambda(lp_ref, lam_init):
    lp = lp_ref[...]
    s01 = jnp.sum(lp[0:1] * lp[1:2], axis=-1, keepdims=True)
    s23 = jnp.sum(lp[2:3] * lp[3:4], axis=-1, keepdims=True)
    return jnp.exp(s01) - jnp.exp(s23) + lam_init


def _subln(o, g_ref, lam_init):
    return _rms(o, g_ref[...]) * (1.0 - lam_init)


def _fold_rows(s, op):
    acc = s[0:SUBLANES]
    for r in range(1, s.shape[0] // SUBLANES):
        acc = op(acc, s[r * SUBLANES:(r + 1) * SUBLANES])
    return acc


def _attn_p_scores(c, qt_ref, k_ref, bias_ref, s_ref):
    tq, tk = ATT_TQ, ATT_TK
    qt = qt_ref[:, c * tq:(c + 1) * tq]
    row = lax.broadcasted_iota(jnp.int32, qt.shape, 0)
    zero = jnp.zeros_like(qt)
    qq = jnp.concatenate([jnp.where(row < DQK, qt, zero), jnp.where(row >= DQK, qt, zero)], axis=1)
    m8 = None
    for j in range(c + 1):
        s = _dot(k_ref[j * tk:(j + 1) * tk, :], qq)
        if c - j < 2:
            s = s + bias_ref[0, c - j]
        s_ref[j] = s
        t = _fold_rows(s, jnp.maximum)
        m8 = t if m8 is None else jnp.maximum(m8, t)
    return jnp.max(m8, axis=0, keepdims=True)


def _attn_p_softmax(c, m, s_ref, p_ref):
    tk = ATT_TK
    for j in range(c + 1):
        p_ref[j * tk:(j + 1) * tk, :] = jnp.exp2(s_ref[j] - m).astype(BF16)


def _attn_p_out(c, lam, vt1_ref, g_ref, o_ref, p_ref, lam_init):
    tq = ATT_TQ
    kend = (c + 1) * ATT_TK
    r = _dot(vt1_ref[:, :kend], p_ref[:kend, :])
    ot = r[:DV] / r[DV:DV + 1]
    o = (ot[:, :tq] - lam * ot[:, tq:]).T
    o_ref[c * tq:(c + 1) * tq, :] = _subln(o, g_ref, lam_init).astype(o_ref.dtype)


def _attn_p_kernel(qt_ref, k_ref, vt_ref, bias_ref, lp_ref, g_ref, o_ref, s_ref, p_ref, vt1_ref, *, lam_init):
    lam = _lambda(lp_ref, lam_init)
    vt1_ref[:DV, :] = vt_ref[...]
    vt1_ref[DV:, :] = jnp.ones((ONES_ROWS, SEQ), BF16)
    nq = SEQ // ATT_TQ
    m = _attn_p_scores(0, qt_ref, k_ref, bias_ref, s_ref.at[0])
    for c in range(nq):
        m_next = None
        if c + 1 < nq:
            m_next = _attn_p_scores(c + 1, qt_ref, k_ref, bias_ref, s_ref.at[(c + 1) % 2])
        _attn_p_softmax(c, m, s_ref.at[c % 2], p_ref.at[c % 2])
        _attn_p_out(c, lam, vt1_ref, g_ref, o_ref, p_ref.at[c % 2], lam_init)
        m = m_next


def _attn_prompt(qt, kb, vt, bias, lp, g, lam_init):
    nk = SEQ // ATT_TK
    return pl.pallas_call(
        functools.partial(_attn_p_kernel, lam_init=lam_init),
        grid=(BATCH, N_HEADS),
        in_specs=[
            pl.BlockSpec((None, DV, SEQ), lambda b, n: (b, n, 0)),
            pl.BlockSpec((SEQ, DV), lambda b, n: (b, n)),
            pl.BlockSpec((None, DV, SEQ), lambda b, n: (b, n, 0)),
            pl.BlockSpec((1, 2, ATT_TK, 2 * ATT_TQ), lambda b, n: (n, 0, 0, 0)),
            pl.BlockSpec((4, DQK), lambda b, n: (0, 0)),
            pl.BlockSpec((1, DV), lambda b, n: (0, 0)),
        ],
        out_specs=pl.BlockSpec((SEQ, DV), lambda b, n: (b, n)),
        out_shape=jax.ShapeDtypeStruct((BATCH * SEQ, D_MODEL), BF16),
        scratch_shapes=[pltpu.VMEM((2, nk, ATT_TK, 2 * ATT_TQ), F32),
                        pltpu.VMEM((2, SEQ, 2 * ATT_TQ), BF16),
                        pltpu.VMEM((DV + ONES_ROWS, SEQ), BF16)],
        compiler_params=_params(("parallel", "parallel")),
        name="attn_prompt",
    )(qt, kb, vt, bias, lp, g)


def _attn_s_kernel(pt_ref, q_ref, *refs, lam_init):
    del pt_ref
    npg = PAGES_PER_STEP
    k_refs = refs[:npg]
    v_refs = refs[npg:2 * npg]
    (knew_ref, vnew_ref, bias_ref, lp_ref, g_ref, o_ref,
     qbd_ref, m_ref, l_ref, acc_ref) = refs[2 * npg:]
    g = pl.program_id(1)
    ng = pl.num_programs(1)
    hrows = 2 * DEC_SEQ

    @pl.when(g == 0)
    def _():
        q = q_ref[...]
        qt = jnp.broadcast_to(q[None], (2 * N_HEADS, DEC_SEQ, D_MODEL)).reshape(QROWS, D_MODEL)
        r = lax.broadcasted_iota(jnp.int32, (QROWS, D_MODEL), 0)
        c = lax.broadcasted_iota(jnp.int32, (QROWS, D_MODEL), 1)
        qbd_ref[...] = jnp.where((c >> 6) == (r >> 3), qt, 0.0).astype(BF16)
        m_ref[...] = jnp.full(m_ref.shape, -jnp.inf, F32)
        l_ref[...] = jnp.zeros_like(l_ref)
        acc_ref[...] = jnp.zeros_like(acc_ref)

    qbd = qbd_ref[...]

    def update(s, v_heads):
        m_old = m_ref[...]
        m_new = jnp.maximum(m_old, jnp.max(s, axis=-1, keepdims=True))
        alpha = jnp.exp(m_old - m_new)
        p = jnp.exp(s - m_new)
        l_ref[...] = alpha * l_ref[...] + jnp.sum(p, axis=-1, keepdims=True)
        pb = p.astype(BF16)
        pv = jnp.concatenate([_dot(pb[n * hrows:(n + 1) * hrows], v_heads[n]) for n in range(N_HEADS)],
                             axis=0)
        acc_ref[...] = alpha * acc_ref[...] + pv
        m_ref[...] = m_new

    ktb = jnp.concatenate([r[0, 0].astype(BF16) for r in k_refs], axis=1)
    s = _dot(qbd, ktb)
    last = jnp.where(g == ng - 1, bias_ref[:, :PAGE_SIZE], 0.0)
    s = jnp.concatenate([s[:, :(npg - 1) * PAGE_SIZE], s[:, (npg - 1) * PAGE_SIZE:] + last], axis=1)
    v_heads = [jnp.concatenate([r[0, 0, pl.ds(n, PAGE_SIZE, stride=N_HEADS), :].astype(BF16)
                                for r in v_refs], axis=0) for n in range(N_HEADS)]
    update(s, v_heads)

    @pl.when(g == ng - 1)
    def _():
        pad = jnp.zeros((PAGE_SIZE - DEC_SEQ, D_MODEL), F32)
        kn = jnp.concatenate([knew_ref[...], pad], axis=0).astype(BF16)
        vn = jnp.concatenate([vnew_ref[...], pad], axis=0).astype(BF16)
        update(_dot_nt(qbd, kn) + bias_ref[:, PAGE_SIZE:],
               [vn[:, n * DV:(n + 1) * DV] for n in range(N_HEADS)])
        o_all = acc_ref[...] / l_ref[...]
        lam = _lambda(lp_ref, lam_init)
        for n in range(N_HEADS):
            o0 = o_all[n * hrows:n * hrows + DEC_SEQ]
            o1 = o_all[n * hrows + DEC_SEQ:(n + 1) * hrows]
            o_ref[:, n * DV:(n + 1) * DV] = _subln(o0 - lam * o1, g_ref, lam_init)


def _attn_sample(page_table, q, cache_kt, cache_v, layer, knew, vnew, bias, lp, g, lam_init):
    npg = PAGES_PER_STEP
    ng = N_PAGES // npg

    def page_spec(i, rows):
        return pl.BlockSpec((1, 1, rows, LANES),
                            lambda b, gg, pt: (layer, pt[b * N_PAGES + gg * npg + i], 0, 0))

    row8 = pl.BlockSpec((DEC_SEQ, D_MODEL), lambda b, gg, pt: (b, 0))
    grid_spec = pltpu.PrefetchScalarGridSpec(
        num_scalar_prefetch=1,
        grid=(DEC_BATCH, ng),
        in_specs=([row8] + [page_spec(i, D_MODEL) for i in range(npg)]
                  + [page_spec(i, PAGE_SIZE * N_HEADS) for i in range(npg)] + [
            row8, row8,
            pl.BlockSpec((QROWS, 2 * PAGE_SIZE), lambda b, gg, pt: (0, 0)),
            pl.BlockSpec((4, DQK), lambda b, gg, pt: (0, 0)),
            pl.BlockSpec((1, DV), lambda b, gg, pt: (0, 0)),
        ]),
        out_specs=row8,
        scratch_shapes=[pltpu.VMEM((QROWS, D_MODEL), BF16),
                        pltpu.VMEM((QROWS, 1), F32),
                        pltpu.VMEM((QROWS, 1), F32),
                        pltpu.VMEM((QROWS, DV), F32)],
    )
    return pl.pallas_call(
        functools.partial(_attn_s_kernel, lam_init=lam_init),
        grid_spec=grid_spec,
        out_shape=jax.ShapeDtypeStruct((DEC_BATCH * DEC_SEQ, D_MODEL), F32),
        compiler_params=_params(("parallel", "arbitrary")),
        name="attn_sample",
    )(page_table, q, *([cache_kt] * npg), *([cache_v] * npg), knew, vnew, bias, lp, g)


def _proj_res_kernel(a_ref, w_ref, x_ref, o_ref):
    o_ref[...] = x_ref[...] + _dot(a_ref[...].astype(BF16), w_ref[...])


def _proj_res(a, w, layer, x, tm):
    m = x.shape[0]
    row = pl.BlockSpec((tm, D_MODEL), lambda i: (i, 0))
    return pl.pallas_call(
        _proj_res_kernel,
        grid=(m // tm,),
        in_specs=[row, pl.BlockSpec((None, D_MODEL, D_MODEL), lambda i: (layer, 0, 0)), row],
        out_specs=row,
        out_shape=jax.ShapeDtypeStruct((m, D_MODEL), F32),
        compiler_params=_params(("parallel",)),
        name="proj_res",
    )(a, w, x)


def _conv_kernel(x_ref, hist_ref, g_ref, w1_ref, b1_ref, wdw_ref, bdw_ref, lng_ref, lnb_ref,
                 w2_ref, b2_ref, o_ref, st_ref, buf_ref, *win_refs, nbb, tt):
    t = pl.program_id(1)

    @pl.when(t == 0)
    def _():
        buf_ref[:, 0:HIST_PAD, :] = hist_ref[...]

    @pl.when(t > 0)
    def _():
        buf_ref[:, 0:HIST_PAD, :] = buf_ref[:, tt:tt + HIST_PAD, :]

    x = x_ref[...]
    h = _rms(x, g_ref[...]).astype(BF16)
    ag = _dot(h, w1_ref[...]) + b1_ref[...]
    u = ag[:, :D_MODEL] * jax.nn.sigmoid(ag[:, D_MODEL:])
    buf_ref[:, HIST_PAD:, :] = u.reshape(nbb, tt, D_MODEL)

    off = HIST_PAD - (CONV_W - 1)
    y = None
    for b in range(SUBLANES):
        es = [e for e in range(off, off + CONV_W) if e % SUBLANES == b]
        if b == 0:
            win_ref = buf_ref
        else:
            win_ref = win_refs[b % 2]
            win_ref[...] = buf_ref[:, b:b + win_ref.shape[1], :]
        for e in es:
            term = win_ref[:, e - b:e - b + tt, :] * wdw_ref[e - off:e - off + 1, :]
            y = term if y is None else y + term
    y = y.reshape(nbb * tt, D_MODEL) + bdw_ref[...]
    mu = jnp.mean(y, axis=-1, keepdims=True)
    yc = y - mu
    z = yc * lax.rsqrt(jnp.mean(yc * yc, axis=-1, keepdims=True) + EPS) * lng_ref[...] + lnb_ref[...]
    z = z * jax.nn.sigmoid(z)
    o_ref[...] = x + _dot(z.astype(BF16), w2_ref[...]) + b2_ref[...]
    st_ref[...] = buf_ref[:, tt:tt + HIST_PAD, :]


def _conv_module(x, hist, g, w1, b1, wdw, bdw, lng, lnb, w2, b2, cidx, nb, nbb, t_len, tt):
    nt = t_len // tt
    row = pl.BlockSpec((nbb * tt, D_MODEL), lambda b, t: (b * nt + t, 0))
    vec = pl.BlockSpec((1, D_MODEL), lambda b, t: (0, 0))
    st = pl.BlockSpec((nbb, HIST_PAD, D_MODEL), lambda b, t: (b, 0, 0))
    return pl.pallas_call(
        functools.partial(_conv_kernel, nbb=nbb, tt=tt),
        grid=(nb // nbb, nt),
        in_specs=[row, st, vec,
                  pl.BlockSpec((None, D_MODEL, 2 * D_MODEL), lambda b, t: (cidx, 0, 0)),
                  pl.BlockSpec((1, 2 * D_MODEL), lambda b, t: (0, 0)),
                  pl.BlockSpec((HIST_PAD, D_MODEL), lambda b, t: (0, 0)),
                  vec, vec, vec,
                  pl.BlockSpec((None, D_MODEL, D_MODEL), lambda b, t: (cidx, 0, 0)),
                  vec],
        out_specs=[row, st],
        out_shape=[jax.ShapeDtypeStruct((nb * t_len, D_MODEL), F32),
                   jax.ShapeDtypeStruct((nb, HIST_PAD, D_MODEL), F32)],
        scratch_shapes=[pltpu.VMEM((nbb, HIST_PAD + tt, D_MODEL), F32)]
                       + [pltpu.VMEM((nbb, HIST_PAD - SUBLANES + tt, D_MODEL), F32)] * 2,
        compiler_params=_params(("parallel", "arbitrary")),
        name="conv_module",
    )(x, hist, g, w1, b1, wdw, bdw, lng, lnb, w2, b2)


def _head_tile_view(a):
    lead, m = a.shape[:-3], a.shape[-3]
    a = a.reshape(lead + (m, N_XHEADS, XHEAD_DIM // LANES, LANES))
    return jnp.swapaxes(a, -3, -2).reshape(lead + (m * HT_ROWS, LANES))


def _from_head_tile_view(x):
    lead, m = x.shape[:-2], x.shape[-2] // HT_ROWS
    x = x.reshape(lead + (m, XHEAD_DIM // LANES, N_XHEADS, LANES))
    return jnp.swapaxes(x, -3, -2).reshape(lead + (m, N_XHEADS, XHEAD_DIM))


def _memkv_kernel(m_ref, w_ref, o_ref, ob_ref):
    y = _dot(m_ref[...].astype(BF16), w_ref[...])
    ob_ref[...] = y.astype(BF16)
    rows = y.shape[0]
    for hd in range(N_XHEADS):
        for c in range(XHEAD_DIM // LANES):
            col = hd * XHEAD_DIM + c * LANES
            o_ref[pl.ds(c * N_XHEADS + hd, rows, stride=HT_ROWS), :] = y[:, col:col + LANES]


def _mem_kv(mem, w_kv, j, tm):
    m = mem.shape[0]
    return pl.pallas_call(
        _memkv_kernel,
        grid=(DEPTH, m // tm),
        in_specs=[pl.BlockSpec((tm, D_MODEL), lambda i, r: (r, 0)),
                  pl.BlockSpec((None, D_MODEL, D_MODEL), lambda i, r: (i, 0, j))],
        out_specs=[pl.BlockSpec((None, tm * HT_ROWS, LANES), lambda i, r: (i, r, 0)),
                   pl.BlockSpec((None, tm, D_MODEL), lambda i, r: (i, r, 0))],
        out_shape=[jax.ShapeDtypeStruct((DEPTH, m * HT_ROWS, LANES), F32),
                   jax.ShapeDtypeStruct((DEPTH, m, D_MODEL), BF16)],
        compiler_params=_params(("parallel", "parallel")),
        name="mem_kv",
    )(mem, w_kv)


def _xhead(q, mk, mv):
    s = _dot_nt(q, mk)
    p = jnp.exp(s - jnp.max(s, axis=-1, keepdims=True))
    l = jnp.sum(p, axis=-1, keepdims=True)
    return _dot(p.astype(BF16), mv) / l


def _xattn_p_kernel(x_ref, g_ref, wq_ref, mk_ref, mv_ref, wo_ref, o_ref):
    x = x_ref[...]
    h = _rms(x, g_ref[...]).astype(BF16)
    q = (_dot(h, wq_ref[...]) * (XHEAD_DIM ** -0.5)).astype(BF16)
    outs = []
    for hd in range(N_XHEADS):
        cs = slice(hd * XHEAD_DIM, (hd + 1) * XHEAD_DIM)
        outs.append(_xhead(q[:, cs], mk_ref[:, cs], mv_ref[:, cs]))
    o = jnp.concatenate(outs, axis=-1)
    o_ref[...] = x + _dot(o.astype(BF16), wo_ref[...])


def _xattn_prompt(x, g, wq, mk_b, mv_b, layer, wo, tm):
    m = x.shape[0]
    per_b = SEQ // tm
    row = pl.BlockSpec((tm, D_MODEL), lambda i: (i, 0))
    wspec = pl.BlockSpec((None, D_MODEL, D_MODEL), lambda i: (layer, 0, 0))
    mem = pl.BlockSpec((None, N_MEM, D_MODEL), lambda i: (layer, i // per_b, 0))
    return pl.pallas_call(
        _xattn_p_kernel,
        grid=(m // tm,),
        in_specs=[row, pl.BlockSpec((1, D_MODEL), lambda i: (0, 0)), wspec, mem, mem, wspec],
        out_specs=row,
        out_shape=jax.ShapeDtypeStruct((m, D_MODEL), F32),
        compiler_params=_params(("parallel",)),
        name="xattn_prompt",
    )(x, g, wq, mk_b, mv_b, wo)


def _xattn_s_kernel(x_ref, g_ref, wq_ref, mk_ref, mv_ref, wo_ref, o_ref, q_ref, a_ref):
    b = pl.program_id(0)

    @pl.when(b == 0)
    def _():
        h = _rms(x_ref[...], g_ref[...]).astype(BF16)
        q_ref[...] = _dot(h, wq_ref[...]) * (XHEAD_DIM ** -0.5)

    r0 = pl.multiple_of(b * DEC_SEQ, DEC_SEQ)
    q = q_ref[pl.ds(r0, DEC_SEQ), :].astype(BF16)
    def head(ref, hd):
        return jnp.concatenate([ref[pl.ds(c * N_XHEADS + hd, N_MEM, stride=HT_ROWS), :]
                                for c in range(XHEAD_DIM // LANES)], axis=1).astype(BF16)

    outs = []
    for hd in range(N_XHEADS):
        outs.append(_xhead(q[:, hd * XHEAD_DIM:(hd + 1) * XHEAD_DIM], head(mk_ref, hd), head(mv_ref, hd)))
    a_ref[pl.ds(r0, DEC_SEQ), :] = jnp.concatenate(outs, axis=-1)

    @pl.when(b == pl.num_programs(0) - 1)
    def _():
        o_ref[...] = x_ref[...] + _dot(a_ref[...].astype(BF16), wo_ref[...])


def _xattn_sample(x, g, wq, mem_k, mem_v, layer, wo):
    m = x.shape[0]
    full = pl.BlockSpec((m, D_MODEL), lambda b: (0, 0))
    wspec = pl.BlockSpec((None, D_MODEL, D_MODEL), lambda b: (layer, 0, 0))
    mem = pl.BlockSpec((None, None, N_MEM * HT_ROWS, LANES), lambda b: (layer, b, 0, 0))
    return pl.pallas_call(
        _xattn_s_kernel,
        grid=(DEC_BATCH,),
        in_specs=[full, pl.BlockSpec((1, D_MODEL), lambda b: (0, 0)), wspec, mem, mem, wspec],
        out_specs=full,
        out_shape=jax.ShapeDtypeStruct((m, D_MODEL), F32),
        scratch_shapes=[pltpu.VMEM((m, D_MODEL), F32), pltpu.VMEM((m, D_MODEL), F32)],
        compiler_params=_params(("arbitrary",)),
        name="xattn_sample",
    )(x, g, wq, mem_k, mem_v, wo)


def _vec(v):
    return v.reshape(1, -1).astype(F32)


def kernel(x_prompt, x_sample, cache_k_diff, cache_v_diff, state_conv, cache_mem_k, cache_mem_v, page_table, mem_prompt, ffn_norm, ffn_w_in, ffn_w_out, mix_norm, diff_w_qkv, diff_lambda, diff_subln, diff_w_o, rel_bias, conv_w_pw1, conv_b_pw1, conv_w_dw, conv_b_dw, conv_ln_g, conv_ln_b, conv_w_pw2, conv_b_pw2, xattn_norm, xattn_w_q, xattn_w_kv, xattn_w_o, final_norm):
    mp = BATCH * SEQ
    ms = DEC_BATCH * DEC_SEQ
    xp = x_prompt.reshape(mp, D_MODEL)
    xs = x_sample.reshape(ms, D_MODEL)
    tm_p, tm_s = ROW_TILE, ms

    w_in = ffn_w_in.astype(BF16)
    w_out = ffn_w_out.astype(BF16)
    wqkv = diff_w_qkv.astype(BF16)
    wqkv_t = jnp.swapaxes(diff_w_qkv, 1, 2).astype(BF16)
    w_do = diff_w_o.astype(BF16)
    w_pw1 = conv_w_pw1.astype(BF16)
    w_pw2 = conv_w_pw2.astype(BF16)
    w_xq = xattn_w_q.astype(BF16)
    w_xkv = xattn_w_kv.astype(BF16)
    w_xo = xattn_w_o.astype(BF16)

    kk = jnp.arange(ATT_TK)[:, None]
    qq = jnp.arange(ATT_TQ)[None, :]
    d0 = qq - kk
    bucket_p = jnp.concatenate([jnp.where(d0 >= 0, _rel_bucket(jnp.maximum(d0, 0)), -1),
                                _rel_bucket(d0 + ATT_TK)], axis=0).astype(jnp.int32)
    bucket_p = jnp.concatenate([bucket_p, bucket_p], axis=1)
    bias_p = _bias_tiles(rel_bias, bucket_p, LOG2E).reshape(N_HEADS, 2, ATT_TK, 2 * ATT_TQ)
    rs = jnp.arange(DEC_SEQ)[:, None]
    cs = jnp.arange(PAGE_SIZE)[None, :]
    d_new = rs - cs
    bucket_s = jnp.concatenate(
        [_rel_bucket(PAGE_SIZE + rs - cs),
         jnp.where((d_new >= 0) & (cs < DEC_SEQ), _rel_bucket(jnp.maximum(d_new, 0)), -1)],
        axis=1).astype(jnp.int32)
    bias_s = _bias_tiles(rel_bias, bucket_s)
    bias_s = jnp.broadcast_to(bias_s[:, None], (N_HEADS, 2, DEC_SEQ, 2 * PAGE_SIZE)).reshape(
        QROWS, 2 * PAGE_SIZE)

    n_pool = cache_k_diff.shape[1]
    ckt = cache_k_diff.transpose(0, 1, 3, 4, 2).reshape(-1, n_pool, D_MODEL, PAGE_SIZE)
    cv = cache_v_diff.reshape(-1, n_pool, PAGE_SIZE * N_HEADS, DV)
    pt = page_table.reshape(-1).astype(jnp.int32)

    mem2d = mem_prompt.reshape(BATCH * N_MEM, D_MODEL)
    mk, mk_b = _mem_kv(mem2d, w_xkv, 0, 512)
    mv, mv_b = _mem_kv(mem2d, w_xkv, 1, 512)
    cmk = _head_tile_view(cache_mem_k)
    cmv = _head_tile_view(cache_mem_v)

    n_attn = (DEPTH + N_MIXERS - 1) // N_MIXERS
    gf = _vec(final_norm)
    kv_p = None
    conv_p, k_s, v_s, conv_s = [], [], [], []
    for i in range(DEPTH):
        g = _vec(ffn_norm[i, 0])
        xp = _ffn(xp, g, w_in, w_out, i, 0, FFN_TM)
        xs = _ffn(xs, g, w_in, w_out, i, 0, tm_s)
        gm = _vec(mix_norm[i])
        if i % N_MIXERS == 0:
            a = i // N_MIXERS
            lam_init = 0.8 - 0.6 * math.exp(-0.3 * i)
            lp = diff_lambda[a].astype(F32)
            gs = _vec(diff_subln[a])
            qt, kt_all, kb, v_all, vt = _qkv_prompt(xp, gm, wqkv, wqkv_t, a, n_attn, QKV_TM, prev=kv_p)
            kv_p = (kt_all, v_all)
            op = _attn_prompt(qt, kb, vt, bias_p, lp, gs, lam_init)
            xp = _proj_res(op, w_do, a, xp, tm_p)
            q, k, v = _qkv_sample(xs, gm, wqkv, a)
            k_s.append(k)
            v_s.append(v)
            os_ = _attn_sample(pt, q, ckt, cv, a, k, v, bias_s, lp, gs, lam_init)
            xs = _proj_res(os_, w_do, a, xs, tm_s)
        else:
            cidx = i // N_MIXERS
            cw = (w_pw1, _vec(conv_b_pw1[cidx]),
                  jnp.pad(conv_w_dw[cidx], ((0, HIST_PAD - CONV_W), (0, 0))), _vec(conv_b_dw[cidx]),
                  _vec(conv_ln_g[cidx]), _vec(conv_ln_b[cidx]),
                  w_pw2, _vec(conv_b_pw2[cidx]))
            pad = HIST_PAD - (CONV_W - 1)
            hist_p = jnp.zeros((BATCH, HIST_PAD, D_MODEL), F32)
            hist_s = jnp.pad(state_conv[cidx], ((0, 0), (pad, 0), (0, 0)))
            xp, st_p = _conv_module(xp, hist_p, gm, *cw, cidx, BATCH, 1, SEQ, 512)
            xs, st_s = _conv_module(xs, hist_s, gm, *cw, cidx, DEC_BATCH, DEC_BATCH, DEC_SEQ, DEC_SEQ)
            conv_p.append(st_p[:, pad:])
            conv_s.append(st_s[:, pad:])
        gx = _vec(xattn_norm[i])
        xp = _xattn_prompt(xp, gx, w_xq, mk_b, mv_b, i, w_xo, 512)
        xs = _xattn_sample(xs, gx, w_xq, cmk, cmv, i, w_xo)
        g = _vec(ffn_norm[i, 1])
        last = gf if i == DEPTH - 1 else None
        xp = _ffn(xp, g, w_in, w_out, i, 1, FFN_TM, final_g=last)
        xs = _ffn(xs, g, w_in, w_out, i, 1, tm_s, final_g=last)

    y_prompt = xp.reshape(BATCH, SEQ, D_MODEL)
    y_sample = xs.reshape(DEC_BATCH, DEC_SEQ, D_MODEL)
    na = n_attn
    kt_all, v_all = kv_p
    mem_shape = (DEPTH, BATCH, N_MEM, N_XHEADS, XHEAD_DIM)
    new_k_prompt = kt_all.reshape(na, BATCH, 2 * N_HEADS, DQK, SEQ).transpose(0, 1, 4, 2, 3)
    return (y_prompt, y_sample,
            new_k_prompt,
            v_all.reshape(na, BATCH, SEQ, N_HEADS, DV),
            jnp.stack(conv_p),
            _from_head_tile_view(mk).reshape(mem_shape),
            _from_head_tile_view(mv).reshape(mem_shape),
            jnp.stack(k_s).reshape(na, DEC_BATCH, DEC_SEQ, 2 * N_HEADS, DQK),
            jnp.stack(v_s).reshape(na, DEC_BATCH, DEC_SEQ, N_HEADS, DV),
            jnp.stack(conv_s))
```

```python
import functools
import math

import jax
import jax.numpy as jnp
from jax import lax
from jax.experimental import pallas as pl
from jax.experimental.pallas import tpu as pltpu

D_MODEL = 1024
BATCH = 8
SEQ = 2048
DEPTH = 4
DEC_BATCH = 32
DEC_SEQ = 8
PAST_LEN = 8192
PAGE_SIZE = 128
N_MIXERS = 2
N_HEADS = 8
DQK = D_MODEL // N_HEADS // 2
DV = 2 * DQK
N_BUCKETS = 32
MAX_DISTANCE = 128
CONV_W = 31
D_FF = 2816
N_MEM = 256
N_XHEADS = 4
XHEAD_DIM = D_MODEL // N_XHEADS
EPS = 1e-6
LOG2E = math.log2(math.e)

F32 = jnp.float32
BF16 = jnp.bfloat16

VMEM_LIMIT = 56 * 1024 * 1024
LANES = 128
SUBLANES = 8
HIST_PAD = 32
FFN_CHUNKS = ((0, 1536), (1536, D_FF))
ROW_TILE = 1024
FFN_TM = 512
QKV_TM = 512
ATT_TQ = 256
ATT_TK = 256
PAGES_PER_STEP = 16
HT_ROWS = N_XHEADS * (XHEAD_DIM // LANES)
N_PAGES = PAST_LEN // PAGE_SIZE
QROWS = 2 * N_HEADS * DEC_SEQ
HROWS = 2 * DEC_SEQ
FUSED_TF = 256
FUSED_PAGES = 6


def _params(sem):
    return pltpu.CompilerParams(dimension_semantics=sem, vmem_limit_bytes=VMEM_LIMIT)


def _rms(x, g):
    return x * lax.rsqrt(jnp.mean(x * x, axis=-1, keepdims=True) + EPS) * g


def _dot(a, b):
    return jnp.dot(a, b, preferred_element_type=F32)


def _dot_nt(a, b):
    return lax.dot_general(a, b, (((1,), (1,)), ((), ())), preferred_element_type=F32)


def _ffn_kernel(x_ref, g_ref, win_ref, wout_ref, *rest, final):
    x = x_ref[...]
    h = _rms(x, g_ref[...]).astype(BF16)
    acc = None
    for c0, c1 in FFN_CHUNKS:
        a = _dot(h, win_ref[:, c0:c1])
        u = _dot(h, win_ref[:, D_FF + c0:D_FF + c1])
        act = ((a * jax.nn.sigmoid(a)) * u).astype(BF16)
        part = _dot(act, wout_ref[c0:c1, :])
        acc = part if acc is None else acc + part
    y = x + 0.5 * acc
    if final:
        gf_ref, o_ref = rest
        o_ref[...] = _rms(y, gf_ref[...])
    else:
        rest[0][...] = y


def _ffn(x, g, w_in, w_out, layer, half, tm, final_g=None):
    m = x.shape[0]
    row = pl.BlockSpec((tm, D_MODEL), lambda i: (i, 0))
    vec = pl.BlockSpec((1, D_MODEL), lambda i: (0, 0))
    resident = dict(pipeline_mode=pl.Buffered(1))
    in_specs = [row, vec,
                pl.BlockSpec((None, None, D_MODEL, 2 * D_FF), lambda i: (layer, half, 0, 0), **resident),
                pl.BlockSpec((None, None, D_FF, D_MODEL), lambda i: (layer, half, 0, 0), **resident)]
    args = [x, g, w_in, w_out]
    if final_g is not None:
        in_specs.append(vec)
        args.append(final_g)
    return pl.pallas_call(
        functools.partial(_ffn_kernel, final=final_g is not None),
        grid=(m // tm,),
        in_specs=in_specs,
        out_specs=row,
        out_shape=jax.ShapeDtypeStruct((m, D_MODEL), F32),
        compiler_params=_params(("parallel",)),
        name="ffn",
    )(*args)


def _qkv_p_kernel(x_ref, g_ref, wqt_ref, wkt_ref, wvt_ref, wk_ref, wv_ref, *rest):
    qt_ref, kt_ref, kb_ref, v_ref, vt_ref = rest[-5:]
    h = _rms(x_ref[...], g_ref[...]).astype(BF16)
    qt_ref[...] = (_dot_nt(wqt_ref[...], h) * (DQK ** -0.5 * LOG2E)).astype(BF16)
    kt_ref[...] = _dot_nt(wkt_ref[...], h)
    vt_ref[...] = _dot_nt(wvt_ref[...], h).astype(BF16)
    kb_ref[...] = _dot(h, wk_ref[...]).astype(BF16)
    v = _dot(h, wv_ref[...])
    rows = v.shape[0]
    for n in range(N_HEADS):
        v_ref[pl.ds(n, rows, stride=N_HEADS), :] = v[:, n * DV:(n + 1) * DV]


def _qkv_prompt(x, g, wqkv, wqkv_t, a, n_layers, tm, prev=None):
    nt = SEQ // tm
    row = pl.BlockSpec((tm, D_MODEL), lambda b, t: (b * nt + t, 0))
    tr = pl.BlockSpec((None, D_MODEL, tm), lambda b, t: (b, 0, t))

    def wspec(j):
        return pl.BlockSpec((None, D_MODEL, D_MODEL), lambda b, t: (a, 0, j), pipeline_mode=pl.Buffered(1))

    def wtspec(j):
        return pl.BlockSpec((None, D_MODEL, D_MODEL), lambda b, t: (a, j, 0), pipeline_mode=pl.Buffered(1))

    in_specs = [row, pl.BlockSpec((1, D_MODEL), lambda b, t: (0, 0)),
                wtspec(0), wtspec(1), wtspec(2), wspec(1), wspec(2)]
    args = [x, g, wqkv_t, wqkv_t, wqkv_t, wqkv, wqkv]
    aliases = {}
    if prev is not None:
        in_specs += [pl.BlockSpec(memory_space=pl.ANY)] * 2
        aliases = {len(args): 1, len(args) + 1: 3}
        args += list(prev)
    tr_sds = jax.ShapeDtypeStruct((BATCH, D_MODEL, SEQ), BF16)
    return pl.pallas_call(
        _qkv_p_kernel,
        grid=(BATCH, nt),
        in_specs=in_specs,
        out_specs=[tr,
                   pl.BlockSpec((None, None, D_MODEL, tm), lambda b, t: (a, b, 0, t)),
                   row,
                   pl.BlockSpec((None, tm * N_HEADS, DV), lambda b, t: (a, b * nt + t, 0)),
                   tr],
        out_shape=[tr_sds,
                   jax.ShapeDtypeStruct((n_layers, BATCH, D_MODEL, SEQ), F32),
                   jax.ShapeDtypeStruct((BATCH * SEQ, D_MODEL), BF16),
                   jax.ShapeDtypeStruct((n_layers, BATCH * SEQ * N_HEADS, DV), F32),
                   tr_sds],
        input_output_aliases=aliases,
        compiler_params=_params(("parallel", "parallel")),
        name="qkv_prompt",
    )(*args)


def _qkv_s_kernel(x_ref, g_ref, w_ref, q_ref, k_ref, v_ref, h_ref):
    n = pl.program_id(0)

    @pl.when(n == 0)
    def _():
        h_ref[...] = _rms(x_ref[...], g_ref[...]).astype(BF16)

    y = _dot(h_ref[...], w_ref[...])

    @pl.when(n == 0)
    def _():
        q_ref[...] = y * (DQK ** -0.5)

    @pl.when(n == 1)
    def _():
        k_ref[...] = y

    @pl.when(n == 2)
    def _():
        v_ref[...] = y


def _qkv_sample(x, g, wqkv, a):
    m = x.shape[0]
    full = pl.BlockSpec((m, D_MODEL), lambda n: (0, 0))
    out = jax.ShapeDtypeStruct((m, D_MODEL), F32)
    return pl.pallas_call(
        _qkv_s_kernel,
        grid=(3,),
        in_specs=[full, pl.BlockSpec((1, D_MODEL), lambda n: (0, 0)),
                  pl.BlockSpec((None, D_MODEL, D_MODEL), lambda n: (a, 0, n))],
        out_specs=[full, full, full],
        out_shape=[out, out, out],
        scratch_shapes=[pltpu.VMEM((m, D_MODEL), BF16)],
        compiler_params=_params(("arbitrary",)),
        name="qkv_sample",
    )(x, g, wqkv)


def _bias_kernel(rb_ref, bucket_ref, o_ref, *, scale):
    n = pl.program_id(0)
    bucket = bucket_ref[...]
    acc = jnp.zeros(bucket.shape, F32)
    for b in range(N_BUCKETS):
        acc = jnp.where(bucket == b, rb_ref[b, n], acc)
    o_ref[0] = jnp.where(bucket < 0, -jnp.inf, (acc - rb_ref[N_BUCKETS - 1, n]) * scale)


def _bias_tiles(rel_bias, bucket, scale=1.0):
    r, c = bucket.shape
    return pl.pallas_call(
        functools.partial(_bias_kernel, scale=scale),
        grid=(N_HEADS,),
        in_specs=[pl.BlockSpec(memory_space=pltpu.SMEM),
                  pl.BlockSpec((r, c), lambda n: (0, 0))],
        out_specs=pl.BlockSpec((1, r, c), lambda n: (n, 0, 0)),
        out_shape=jax.ShapeDtypeStruct((N_HEADS, r, c), F32),
        compiler_params=_params(("arbitrary",)),
        name="bias_tiles",
    )(rel_bias, bucket)


def _rel_bucket(n):
    max_exact = N_BUCKETS // 2
    nf = jnp.maximum(n, 1).astype(F32)
    large = max_exact + (jnp.log(nf / max_exact) / math.log(MAX_DISTANCE / max_exact)
                         * (N_BUCKETS - max_exact)).astype(jnp.int32)
    large = jnp.minimum(large, N_BUCKETS - 1)
    return jnp.where(n < max_exact, n, large)


def _lambda(lp_ref, lam_init):
    lp = lp_ref[...]
    s01 = jnp.sum(lp[0:1] * lp[1:2], axis=-1, keepdims=True)
    s23 = jnp.sum(lp[2:3] * lp[3:4], axis=-1, keepdims=True)
    return jnp.exp(s01) - jnp.exp(s23) + lam_init


def _subln(o, g_ref, lam_init):
    return _rms(o, g_ref[...]) * (1.0 - lam_init)


def _fold_rows(s, op):
    acc = s[0:SUBLANES]
    for r in range(1, s.shape[0] // SUBLANES):
        acc = op(acc, s[r * SUBLANES:(r + 1) * SUBLANES])
    return acc


def _attn_p_scores(c, qt_ref, k_ref, bias_ref, s_ref):
    tq, tk = ATT_TQ, ATT_TK
    qt = qt_ref[:, c * tq:(c + 1) * tq]
    row = lax.broadcasted_iota(jnp.int32, qt.shape, 0)
    zero = jnp.zeros_like(qt)
    qq = jnp.concatenate([jnp.where(row < DQK, qt, zero), jnp.where(row >= DQK, qt, zero)], axis=1)
    m8 = None
    for j in range(c + 1):
        s = _dot(k_ref[j * tk:(j + 1) * tk, :], qq)
        if c - j < 2:
            s = s + bias_ref[0, c - j]
        s_ref[j] = s
        t = _fold_rows(s, jnp.maximum)
        m8 = t if m8 is None else jnp.maximum(m8, t)
    return jnp.max(m8, axis=0, keepdims=True)


def _attn_p_softmax(c, m, s_ref, p_ref):
    tk = ATT_TK
    l8 = None
    for j in range(c + 1):
        p = jnp.exp2(s_ref[j] - m)
        t = _fold_rows(p, jnp.add)
        l8 = t if l8 is None else l8 + t
        p_ref[j * tk:(j + 1) * tk, :] = p.astype(BF16)
    return jnp.sum(l8, axis=0, keepdims=True)


def _attn_p_out(c, l, lam, vt_ref, g_ref, o_ref, p_ref, lam_init):
    tq = ATT_TQ
    kend = (c + 1) * ATT_TK
    ot = _dot(vt_ref[:, :kend], p_ref[:kend, :]) / l
    o = (ot[:, :tq] - lam * ot[:, tq:]).T
    o_ref[c * tq:(c + 1) * tq, :] = _subln(o, g_ref, lam_init).astype(o_ref.dtype)


def _attn_p_kernel(qt_ref, k_ref, vt_ref, bias_ref, lp_ref, g_ref, o_ref, s_ref, p_ref, *, lam_init):
    lam = _lambda(lp_ref, lam_init)
    nq = SEQ // ATT_TQ
    m = _attn_p_scores(0, qt_ref, k_ref, bias_ref, s_ref.at[0])
    for c in range(nq):
        m_next = None
        if c + 1 < nq:
            m_next = _attn_p_scores(c + 1, qt_ref, k_ref, bias_ref, s_ref.at[(c + 1) % 2])
        l = _attn_p_softmax(c, m, s_ref.at[c % 2], p_ref.at[c % 2])
        _attn_p_out(c, l, lam, vt_ref, g_ref, o_ref, p_ref.at[c % 2], lam_init)
        m = m_next


def _attn_prompt(qt, kb, vt, bias, lp, g, lam_init):
    nk = SEQ // ATT_TK
    return pl.pallas_call(
        functools.partial(_attn_p_kernel, lam_init=lam_init),
        grid=(BATCH, N_HEADS),
        in_specs=[
            pl.BlockSpec((None, DV, SEQ), lambda b, n: (b, n, 0)),
            pl.BlockSpec((SEQ, DV), lambda b, n: (b, n)),
            pl.BlockSpec((None, DV, SEQ), lambda b, n: (b, n, 0)),
            pl.BlockSpec((1, 2, ATT_TK, 2 * ATT_TQ), lambda b, n: (n, 0, 0, 0)),
            pl.BlockSpec((4, DQK), lambda b, n: (0, 0)),
            pl.BlockSpec((1, DV), lambda b, n: (0, 0)),
        ],
        out_specs=pl.BlockSpec((SEQ, DV), lambda b, n: (b, n)),
        out_shape=jax.ShapeDtypeStruct((BATCH * SEQ, D_MODEL), BF16),
        scratch_shapes=[pltpu.VMEM((2, nk, ATT_TK, 2 * ATT_TQ), F32),
                        pltpu.VMEM((2, SEQ, 2 * ATT_TQ), BF16)],
        compiler_params=_params(("parallel", "parallel")),
        name="attn_prompt",
    )(qt, kb, vt, bias, lp, g)


def _attn_s_init(q_ref, qbd_ref, m_ref, l_ref, acc_ref):
    q = q_ref[...]
    qt = jnp.broadcast_to(q[None], (2 * N_HEADS, DEC_SEQ, D_MODEL)).reshape(QROWS, D_MODEL)
    r = lax.broadcasted_iota(jnp.int32, (QROWS, D_MODEL), 0)
    c = lax.broadcasted_iota(jnp.int32, (QROWS, D_MODEL), 1)
    qbd_ref[...] = jnp.where((c >> 6) == (r >> 3), qt, 0.0).astype(BF16)
    m_ref[...] = jnp.full(m_ref.shape, -jnp.inf, F32)
    l_ref[...] = jnp.zeros_like(l_ref)
    acc_ref[...] = jnp.zeros_like(acc_ref)


def _attn_s_update(s, v_heads, m_ref, l_ref, acc_ref):
    m_old = m_ref[...]
    m_new = jnp.maximum(m_old, jnp.max(s, axis=-1, keepdims=True))
    alpha = jnp.exp(m_old - m_new)
    p = jnp.exp(s - m_new)
    l_ref[...] = alpha * l_ref[...] + jnp.sum(p, axis=-1, keepdims=True)
    pb = p.astype(BF16)
    pv = jnp.concatenate([_dot(pb[n * HROWS:(n + 1) * HROWS], v_heads[n]) for n in range(N_HEADS)], axis=0)
    acc_ref[...] = alpha * acc_ref[...] + pv
    m_ref[...] = m_new


def _attn_s_pages(qbd, k_refs, v_refs):
    ktb = jnp.concatenate([r[0, 0].astype(BF16) for r in k_refs], axis=1)
    s = _dot(qbd, ktb)
    v_heads = [jnp.concatenate([r[0, 0, pl.ds(n, PAGE_SIZE, stride=N_HEADS), :].astype(BF16)
                                for r in v_refs], axis=0) for n in range(N_HEADS)]
    return s, v_heads


def _attn_s_finish(qbd, knew_ref, vnew_ref, bias_ref, lp_ref, g_ref, o_ref, m_ref, l_ref, acc_ref, lam_init):
    pad = jnp.zeros((PAGE_SIZE - DEC_SEQ, D_MODEL), F32)
    kn = jnp.concatenate([knew_ref[...], pad], axis=0).astype(BF16)
    vn = jnp.concatenate([vnew_ref[...], pad], axis=0).astype(BF16)
    _attn_s_update(_dot_nt(qbd, kn) + bias_ref[:, PAGE_SIZE:],
                   [vn[:, n * DV:(n + 1) * DV] for n in range(N_HEADS)], m_ref, l_ref, acc_ref)
    o_all = acc_ref[...] / l_ref[...]
    lam = _lambda(lp_ref, lam_init)
    for n in range(N_HEADS):
        o0 = o_all[n * HROWS:n * HROWS + DEC_SEQ]
        o1 = o_all[n * HROWS + DEC_SEQ:(n + 1) * HROWS]
        o_ref[:, n * DV:(n + 1) * DV] = _subln(o0 - lam * o1, g_ref, lam_init)


def _attn_s_kernel(pt_ref, q_ref, *refs, lam_init):
    del pt_ref
    npg = PAGES_PER_STEP
    k_refs = refs[:npg]
    v_refs = refs[npg:2 * npg]
    (knew_ref, vnew_ref, bias_ref, lp_ref, g_ref, o_ref,
     qbd_ref, m_ref, l_ref, acc_ref) = refs[2 * npg:]
    g = pl.program_id(1)
    ng = pl.num_programs(1)

    pl.when(g == 0)(functools.partial(_attn_s_init, q_ref, qbd_ref, m_ref, l_ref, acc_ref))
    qbd = qbd_ref[...]
    s, v_heads = _attn_s_pages(qbd, k_refs, v_refs)
    last = jnp.where(g == ng - 1, bias_ref[:, :PAGE_SIZE], 0.0)
    s = jnp.concatenate([s[:, :(npg - 1) * PAGE_SIZE], s[:, (npg - 1) * PAGE_SIZE:] + last], axis=1)
    _attn_s_update(s, v_heads, m_ref, l_ref, acc_ref)
    pl.when(g == ng - 1)(functools.partial(_attn_s_finish, qbd, knew_ref, vnew_ref, bias_ref, lp_ref, g_ref,
                                           o_ref, m_ref, l_ref, acc_ref, lam_init))


def _ffn_attn_kernel(pt_ref, x_ref, gf_ref, win_ref, wout_ref, q_ref, *refs, lam_init):
    del pt_ref
    npg = FUSED_PAGES
    k_refs = refs[:npg]
    v_refs = refs[npg:2 * npg]
    (knew_ref, vnew_ref, bias_ref, lp_ref, g_ref, o_ref, os_ref,
     h_ref, facc_ref, qbd_ref, m_ref, l_ref, acc_ref) = refs[2 * npg:]
    f = pl.program_id(1)
    nf = pl.num_programs(1)
    nchunk = D_FF // FUSED_TF

    @pl.when(f == 0)
    def _():
        h_ref[...] = _rms(x_ref[...], gf_ref[...]).astype(BF16)
        facc_ref[...] = jnp.zeros_like(facc_ref)
        _attn_s_init(q_ref, qbd_ref, m_ref, l_ref, acc_ref)

    c0 = pl.multiple_of(f * FUSED_TF, FUSED_TF)
    h = h_ref[...]
    a = _dot(h, win_ref[:, pl.ds(c0, FUSED_TF)])
    u = _dot(h, win_ref[:, pl.ds(pl.multiple_of(D_FF + f * FUSED_TF, FUSED_TF), FUSED_TF)])
    act = ((a * jax.nn.sigmoid(a)) * u).astype(BF16)
    facc_ref[...] += _dot(act, wout_ref[pl.ds(c0, FUSED_TF), :])

    qbd = qbd_ref[...]
    s, v_heads = _attn_s_pages(qbd, k_refs, v_refs)
    cols = []
    for t in range(npg):
        st = s[:, t * PAGE_SIZE:(t + 1) * PAGE_SIZE]
        slot = f * npg + t
        if t == (N_PAGES - 1) % npg:
            st = st + jnp.where(slot == N_PAGES - 1, bias_ref[:, :PAGE_SIZE], 0.0)
        if (nchunk - 1) * npg + t >= N_PAGES:
            st = jnp.where(slot < N_PAGES, st, -jnp.inf)
        cols.append(st)
    _attn_s_update(jnp.concatenate(cols, axis=1), v_heads, m_ref, l_ref, acc_ref)

    @pl.when(f == nf - 1)
    def _():
        o_ref[...] = x_ref[...] + 0.5 * facc_ref[...]
        _attn_s_finish(qbd, knew_ref, vnew_ref, bias_ref, lp_ref, g_ref, os_ref, m_ref, l_ref, acc_ref, lam_init)


def _ffn_attn(x, gf, w_in, w_out, layer, half, page_table, q, cache_kt, cache_v, a, knew, vnew, bias, lp, g,
              lam_init):
    npg = FUSED_PAGES
    nf = D_FF // FUSED_TF
    tm = x.shape[0] // DEC_BATCH

    def page_spec(t, rows):
        last_valid = (N_PAGES - 1 - t) // npg * npg + t
        def index(b, ff, pt):
            return (a, pt[b * N_PAGES + jnp.minimum(ff * npg + t, last_valid)], 0, 0)
        return pl.BlockSpec((1, 1, rows, LANES), index)

    row = pl.BlockSpec((tm, D_MODEL), lambda b, ff, pt: (b, 0))
    row8 = pl.BlockSpec((DEC_SEQ, D_MODEL), lambda b, ff, pt: (b, 0))
    resident = dict(pipeline_mode=pl.Buffered(1))
    grid_spec = pltpu.PrefetchScalarGridSpec(
        num_scalar_prefetch=1,
        grid=(DEC_BATCH, nf),
        in_specs=([row, pl.BlockSpec((1, D_MODEL), lambda b, ff, pt: (0, 0)),
                   pl.BlockSpec((None, None, D_MODEL, 2 * D_FF), lambda b, ff, pt: (layer, half, 0, 0), **resident),
                   pl.BlockSpec((None, None, D_FF, D_MODEL), lambda b, ff, pt: (layer, half, 0, 0), **resident),
                   row8]
                  + [page_spec(t, D_MODEL) for t in range(npg)]
                  + [page_spec(t, PAGE_SIZE * N_HEADS) for t in range(npg)] + [
            row8, row8,
            pl.BlockSpec((QROWS, 2 * PAGE_SIZE), lambda b, ff, pt: (0, 0)),
            pl.BlockSpec((4, DQK), lambda b, ff, pt: (0, 0)),
            pl.BlockSpec((1, DV), lambda b, ff, pt: (0, 0)),
        ]),
        out_specs=[row, row8],
        scratch_shapes=[pltpu.VMEM((tm, D_MODEL), BF16),
                        pltpu.VMEM((tm, D_MODEL), F32),
                        pltpu.VMEM((QROWS, D_MODEL), BF16),
                        pltpu.VMEM((QROWS, 1), F32),
                        pltpu.VMEM((QROWS, 1), F32),
                        pltpu.VMEM((QROWS, DV), F32)],
    )
    return pl.pallas_call(
        functools.partial(_ffn_attn_kernel, lam_init=lam_init),
        grid_spec=grid_spec,
        out_shape=[jax.ShapeDtypeStruct(x.shape, F32),
                   jax.ShapeDtypeStruct((DEC_BATCH * DEC_SEQ, D_MODEL), F32)],
        compiler_params=_params(("parallel", "arbitrary")),
        name="ffn_attn_sample",
    )(page_table, x, gf, w_in, w_out, q, *([cache_kt] * npg), *([cache_v] * npg), knew, vnew, bias, lp, g)


def _attn_sample(page_table, q, cache_kt, cache_v, layer, knew, vnew, bias, lp, g, lam_init):
    npg = PAGES_PER_STEP
    ng = N_PAGES // npg

    def page_spec(i, rows):
        return pl.BlockSpec((1, 1, rows, LANES),
                            lambda b, gg, pt: (layer, pt[b * N_PAGES + gg * npg + i], 0, 0))

    row8 = pl.BlockSpec((DEC_SEQ, D_MODEL), lambda b, gg, pt: (b, 0))
    grid_spec = pltpu.PrefetchScalarGridSpec(
        num_scalar_prefetch=1,
        grid=(DEC_BATCH, ng),
        in_specs=([row8] + [page_spec(i, D_MODEL) for i in range(npg)]
                  + [page_spec(i, PAGE_SIZE * N_HEADS) for i in range(npg)] + [
            row8, row8,
            pl.BlockSpec((QROWS, 2 * PAGE_SIZE), lambda b, gg, pt: (0, 0)),
            pl.BlockSpec((4, DQK), lambda b, gg, pt: (0, 0)),
            pl.BlockSpec((1, DV), lambda b, gg, pt: (0, 0)),
        ]),
        out_specs=row8,
        scratch_shapes=[pltpu.VMEM((QROWS, D_MODEL), BF16),
                        pltpu.VMEM((QROWS, 1), F32),
                        pltpu.VMEM((QROWS, 1), F32),
                        pltpu.VMEM((QROWS, DV), F32)],
    )
    return pl.pallas_call(
        functools.partial(_attn_s_kernel, lam_init=lam_init),
        grid_spec=grid_spec,
        out_shape=jax.ShapeDtypeStruct((DEC_BATCH * DEC_SEQ, D_MODEL), F32),
        compiler_params=_params(("parallel", "arbitrary")),
        name="attn_sample",
    )(page_table, q, *([cache_kt] * npg), *([cache_v] * npg), knew, vnew, bias, lp, g)


def _proj_res_kernel(a_ref, w_ref, x_ref, o_ref):
    o_ref[...] = x_ref[...] + _dot(a_ref[...].astype(BF16), w_ref[...])


def _proj_res(a, w, layer, x, tm):
    m = x.shape[0]
    row = pl.BlockSpec((tm, D_MODEL), lambda i: (i, 0))
    return pl.pallas_call(
        _proj_res_kernel,
        grid=(m // tm,),
        in_specs=[row, pl.BlockSpec((None, D_MODEL, D_MODEL), lambda i: (layer, 0, 0)), row],
        out_specs=row,
        out_shape=jax.ShapeDtypeStruct((m, D_MODEL), F32),
        compiler_params=_params(("parallel",)),
        name="proj_res",
    )(a, w, x)


def _conv_kernel(x_ref, hist_ref, g_ref, w1_ref, b1_ref, wdw_ref, bdw_ref, lng_ref, lnb_ref,
                 w2_ref, b2_ref, o_ref, st_ref, buf_ref, *win_refs, nbb, tt):
    t = pl.program_id(1)

    @pl.when(t == 0)
    def _():
        buf_ref[:, 0:HIST_PAD, :] = hist_ref[...]

    @pl.when(t > 0)
    def _():
        buf_ref[:, 0:HIST_PAD, :] = buf_ref[:, tt:tt + HIST_PAD, :]

    x = x_ref[...]
    h = _rms(x, g_ref[...]).astype(BF16)
    ag = _dot(h, w1_ref[...]) + b1_ref[...]
    u = ag[:, :D_MODEL] * jax.nn.sigmoid(ag[:, D_MODEL:])
    buf_ref[:, HIST_PAD:, :] = u.reshape(nbb, tt, D_MODEL)

    off = HIST_PAD - (CONV_W - 1)
    y = None
    for b in range(SUBLANES):
        es = [e for e in range(off, off + CONV_W) if e % SUBLANES == b]
        if b == 0:
            win_ref = buf_ref
        else:
            win_ref = win_refs[b % 2]
            win_ref[...] = buf_ref[:, b:b + win_ref.shape[1], :]
        for e in es:
            term = win_ref[:, e - b:e - b + tt, :] * wdw_ref[e - off:e - off + 1, :]
            y = term if y is None else y + term
    y = y.reshape(nbb * tt, D_MODEL) + bdw_ref[...]
    mu = jnp.mean(y, axis=-1, keepdims=True)
    yc = y - mu
    z = yc * lax.rsqrt(jnp.mean(yc * yc, axis=-1, keepdims=True) + EPS) * lng_ref[...] + lnb_ref[...]
    z = z * jax.nn.sigmoid(z)
    o_ref[...] = x + _dot(z.astype(BF16), w2_ref[...]) + b2_ref[...]
    st_ref[...] = buf_ref[:, tt:tt + HIST_PAD, :]


def _conv_module(x, hist, g, w1, b1, wdw, bdw, lng, lnb, w2, b2, cidx, nb, nbb, t_len, tt):
    nt = t_len // tt
    row = pl.BlockSpec((nbb * tt, D_MODEL), lambda b, t: (b * nt + t, 0))
    vec = pl.BlockSpec((1, D_MODEL), lambda b, t: (0, 0))
    st = pl.BlockSpec((nbb, HIST_PAD, D_MODEL), lambda b, t: (b, 0, 0))
    return pl.pallas_call(
        functools.partial(_conv_kernel, nbb=nbb, tt=tt),
        grid=(nb // nbb, nt),
        in_specs=[row, st, vec,
                  pl.BlockSpec((None, D_MODEL, 2 * D_MODEL), lambda b, t: (cidx, 0, 0)),
                  pl.BlockSpec((1, 2 * D_MODEL), lambda b, t: (0, 0)),
                  pl.BlockSpec((HIST_PAD, D_MODEL), lambda b, t: (0, 0)),
                  vec, vec, vec,
                  pl.BlockSpec((None, D_MODEL, D_MODEL), lambda b, t: (cidx, 0, 0)),
                  vec],
        out_specs=[row, st],
        out_shape=[jax.ShapeDtypeStruct((nb * t_len, D_MODEL), F32),
                   jax.ShapeDtypeStruct((nb, HIST_PAD, D_MODEL), F32)],
        scratch_shapes=[pltpu.VMEM((nbb, HIST_PAD + tt, D_MODEL), F32)]
                       + [pltpu.VMEM((nbb, HIST_PAD - SUBLANES + tt, D_MODEL), F32)] * 2,
        compiler_params=_params(("parallel", "arbitrary")),
        name="conv_module",
    )(x, hist, g, w1, b1, wdw, bdw, lng, lnb, w2, b2)


def _head_tile_view(a):
    lead, m = a.shape[:-3], a.shape[-3]
    a = a.reshape(lead + (m, N_XHEADS, XHEAD_DIM // LANES, LANES))
    return jnp.swapaxes(a, -3, -2).reshape(lead + (m * HT_ROWS, LANES))


def _from_head_tile_view(x):
    lead, m = x.shape[:-2], x.shape[-2] // HT_ROWS
    x = x.reshape(lead + (m, XHEAD_DIM // LANES, N_XHEADS, LANES))
    return jnp.swapaxes(x, -3, -2).reshape(lead + (m, N_XHEADS, XHEAD_DIM))


def _memkv_kernel(m_ref, w_ref, o_ref, ob_ref):
    y = _dot(m_ref[...].astype(BF16), w_ref[...])
    ob_ref[...] = y.astype(BF16)
    rows = y.shape[0]
    for hd in range(N_XHEADS):
        for c in range(XHEAD_DIM // LANES):
            col = hd * XHEAD_DIM + c * LANES
            o_ref[pl.ds(c * N_XHEADS + hd, rows, stride=HT_ROWS), :] = y[:, col:col + LANES]


def _mem_kv(mem, w_kv, j, tm):
    m = mem.shape[0]
    return pl.pallas_call(
        _memkv_kernel,
        grid=(DEPTH, m // tm),
        in_specs=[pl.BlockSpec((tm, D_MODEL), lambda i, r: (r, 0)),
                  pl.BlockSpec((None, D_MODEL, D_MODEL), lambda i, r: (i, 0, j))],
        out_specs=[pl.BlockSpec((None, tm * HT_ROWS, LANES), lambda i, r: (i, r, 0)),
                   pl.BlockSpec((None, tm, D_MODEL), lambda i, r: (i, r, 0))],
        out_shape=[jax.ShapeDtypeStruct((DEPTH, m * HT_ROWS, LANES), F32),
                   jax.ShapeDtypeStruct((DEPTH, m, D_MODEL), BF16)],
        compiler_params=_params(("parallel", "parallel")),
        name="mem_kv",
    )(mem, w_kv)


def _xhead(q, mk, mv):
    s = _dot_nt(q, mk)
    p = jnp.exp(s - jnp.max(s, axis=-1, keepdims=True))
    l = jnp.sum(p, axis=-1, keepdims=True)
    return _dot(p.astype(BF16), mv) / l


def _xattn_p_kernel(x_ref, g_ref, wq_ref, mk_ref, mv_ref, wo_ref, o_ref):
    x = x_ref[...]
    h = _rms(x, g_ref[...]).astype(BF16)
    q = (_dot(h, wq_ref[...]) * (XHEAD_DIM ** -0.5)).astype(BF16)
    outs = []
    for hd in range(N_XHEADS):
        cs = slice(hd * XHEAD_DIM, (hd + 1) * XHEAD_DIM)
        outs.append(_xhead(q[:, cs], mk_ref[:, cs], mv_ref[:, cs]))
    o = jnp.concatenate(outs, axis=-1)
    o_ref[...] = x + _dot(o.astype(BF16), wo_ref[...])


def _xattn_prompt(x, g, wq, mk_b, mv_b, layer, wo, tm):
    m = x.shape[0]
    per_b = SEQ // tm
    row = pl.BlockSpec((tm, D_MODEL), lambda i: (i, 0))
    wspec = pl.BlockSpec((None, D_MODEL, D_MODEL), lambda i: (layer, 0, 0))
    mem = pl.BlockSpec((None, N_MEM, D_MODEL), lambda i: (layer, i // per_b, 0))
    return pl.pallas_call(
        _xattn_p_kernel,
        grid=(m // tm,),
        in_specs=[row, pl.BlockSpec((1, D_MODEL), lambda i: (0, 0)), wspec, mem, mem, wspec],
        out_specs=row,
        out_shape=jax.ShapeDtypeStruct((m, D_MODEL), F32),
        compiler_params=_params(("parallel",)),
        name="xattn_prompt",
    )(x, g, wq, mk_b, mv_b, wo)


def _xattn_s_kernel(x_ref, g_ref, wq_ref, mk_ref, mv_ref, wo_ref, o_ref, q_ref, a_ref):
    b = pl.program_id(0)

    @pl.when(b == 0)
    def _():
        h = _rms(x_ref[...], g_ref[...]).astype(BF16)
        q_ref[...] = _dot(h, wq_ref[...]) * (XHEAD_DIM ** -0.5)

    r0 = pl.multiple_of(b * DEC_SEQ, DEC_SEQ)
    q = q_ref[pl.ds(r0, DEC_SEQ), :].astype(BF16)
    def head(ref, hd):
        return jnp.concatenate([ref[pl.ds(c * N_XHEADS + hd, N_MEM, stride=HT_ROWS), :]
                                for c in range(XHEAD_DIM // LANES)], axis=1).astype(BF16)

    outs = []
    for hd in range(N_XHEADS):
        outs.append(_xhead(q[:, hd * XHEAD_DIM:(hd + 1) * XHEAD_DIM], head(mk_ref, hd), head(mv_ref, hd)))
    a_ref[pl.ds(r0, DEC_SEQ), :] = jnp.concatenate(outs, axis=-1)

    @pl.when(b == pl.num_programs(0) - 1)
    def _():
        o_ref[...] = x_ref[...] + _dot(a_ref[...].astype(BF16), wo_ref[...])


def _xattn_sample(x, g, wq, mem_k, mem_v, layer, wo):
    m = x.shape[0]
    full = pl.BlockSpec((m, D_MODEL), lambda b: (0, 0))
    wspec = pl.BlockSpec((None, D_MODEL, D_MODEL), lambda b: (layer, 0, 0))
    mem = pl.BlockSpec((None, None, N_MEM * HT_ROWS, LANES), lambda b: (layer, b, 0, 0))
    return pl.pallas_call(
        _xattn_s_kernel,
        grid=(DEC_BATCH,),
        in_specs=[full, pl.BlockSpec((1, D_MODEL), lambda b: (0, 0)), wspec, mem, mem, wspec],
        out_specs=full,
        out_shape=jax.ShapeDtypeStruct((m, D_MODEL), F32),
        scratch_shapes=[pltpu.VMEM((m, D_MODEL), F32), pltpu.VMEM((m, D_MODEL), F32)],
        compiler_params=_params(("arbitrary",)),
        name="xattn_sample",
    )(x, g, wq, mem_k, mem_v, wo)


def _vec(v):
    return v.reshape(1, -1).astype(F32)


def kernel(x_prompt, x_sample, cache_k_diff, cache_v_diff, state_conv, cache_mem_k, cache_mem_v, page_table, mem_prompt, ffn_norm, ffn_w_in, ffn_w_out, mix_norm, diff_w_qkv, diff_lambda, diff_subln, diff_w_o, rel_bias, conv_w_pw1, conv_b_pw1, conv_w_dw, conv_b_dw, conv_ln_g, conv_ln_b, conv_w_pw2, conv_b_pw2, xattn_norm, xattn_w_q, xattn_w_kv, xattn_w_o, final_norm):
    mp = BATCH * SEQ
    ms = DEC_BATCH * DEC_SEQ
    xp = x_prompt.reshape(mp, D_MODEL)
    xs = x_sample.reshape(ms, D_MODEL)
    tm_p, tm_s = ROW_TILE, ms

    w_in = ffn_w_in.astype(BF16)
    w_out = ffn_w_out.astype(BF16)
    wqkv = diff_w_qkv.astype(BF16)
    wqkv_t = jnp.swapaxes(diff_w_qkv, 1, 2).astype(BF16)
    w_do = diff_w_o.astype(BF16)
    w_pw1 = conv_w_pw1.astype(BF16)
    w_pw2 = conv_w_pw2.astype(BF16)
    w_xq = xattn_w_q.astype(BF16)
    w_xkv = xattn_w_kv.astype(BF16)
    w_xo = xattn_w_o.astype(BF16)

    kk = jnp.arange(ATT_TK)[:, None]
    qq = jnp.arange(ATT_TQ)[None, :]
    d0 = qq - kk
    bucket_p = jnp.concatenate([jnp.where(d0 >= 0, _rel_bucket(jnp.maximum(d0, 0)), -1),
                                _rel_bucket(d0 + ATT_TK)], axis=0).astype(jnp.int32)
    bucket_p = jnp.concatenate([bucket_p, bucket_p], axis=1)
    bias_p = _bias_tiles(rel_bias, bucket_p, LOG2E).reshape(N_HEADS, 2, ATT_TK, 2 * ATT_TQ)
    rs = jnp.arange(DEC_SEQ)[:, None]
    cs = jnp.arange(PAGE_SIZE)[None, :]
    d_new = rs - cs
    bucket_s = jnp.concatenate(
        [_rel_bucket(PAGE_SIZE + rs - cs),
         jnp.where((d_new >= 0) & (cs < DEC_SEQ), _rel_bucket(jnp.maximum(d_new, 0)), -1)],
        axis=1).astype(jnp.int32)
    bias_s = _bias_tiles(rel_bias, bucket_s)
    bias_s = jnp.broadcast_to(bias_s[:, None], (N_HEADS, 2, DEC_SEQ, 2 * PAGE_SIZE)).reshape(
        QROWS, 2 * PAGE_SIZE)

    n_pool = cache_k_diff.shape[1]
    ckt = cache_k_diff.transpose(0, 1, 3, 4, 2).reshape(-1, n_pool, D_MODEL, PAGE_SIZE)
    cv = cache_v_diff.reshape(-1, n_pool, PAGE_SIZE * N_HEADS, DV)
    pt = page_table.reshape(-1).astype(jnp.int32)

    mem2d = mem_prompt.reshape(BATCH * N_MEM, D_MODEL)
    mk, mk_b = _mem_kv(mem2d, w_xkv, 0, 512)
    mv, mv_b = _mem_kv(mem2d, w_xkv, 1, 512)
    cmk = _head_tile_view(cache_mem_k)
    cmv = _head_tile_view(cache_mem_v)

    n_attn = (DEPTH + N_MIXERS - 1) // N_MIXERS
    gf = _vec(final_norm)
    kv_p = None
    conv_p, k_s, v_s, conv_s = [], [], [], []
    for i in range(DEPTH):
        g = _vec(ffn_norm[i, 0])
        xs = _ffn(xs, g, w_in, w_out, i, 0, tm_s)
        gm = _vec(mix_norm[i])
        if i % N_MIXERS == 0:
            a = i // N_MIXERS
            lam_init = 0.8 - 0.6 * math.exp(-0.3 * i)
            lp = diff_lambda[a].astype(F32)
            gs = _vec(diff_subln[a])
            q, k, v = _qkv_sample(xs, gm, wqkv, a)
            k_s.append(k)
            v_s.append(v)
            xp, os_ = _ffn_attn(xp, g, w_in, w_out, i, 0, pt, q, ckt, cv, a, k, v, bias_s, lp, gs, lam_init)
            xs = _proj_res(os_, w_do, a, xs, tm_s)
            qt, kt_all, kb, v_all, vt = _qkv_prompt(xp, gm, wqkv, wqkv_t, a, n_attn, QKV_TM, prev=kv_p)
            kv_p = (kt_all, v_all)
            op = _attn_prompt(qt, kb, vt, bias_p, lp, gs, lam_init)
            xp = _proj_res(op, w_do, a, xp, tm_p)
        else:
            xp = _ffn(xp, g, w_in, w_out, i, 0, FFN_TM)
            cidx = i // N_MIXERS
            cw = (w_pw1, _vec(conv_b_pw1[cidx]),
                  jnp.pad(conv_w_dw[cidx], ((0, HIST_PAD - CONV_W), (0, 0))), _vec(conv_b_dw[cidx]),
                  _vec(conv_ln_g[cidx]), _vec(conv_ln_b[cidx]),
                  w_pw2, _vec(conv_b_pw2[cidx]))
            pad = HIST_PAD - (CONV_W - 1)
            hist_p = jnp.zeros((BATCH, HIST_PAD, D_MODEL), F32)
            hist_s = jnp.pad(state_conv[cidx], ((0, 0), (pad, 0), (0, 0)))
            xp, st_p = _conv_module(xp, hist_p, gm, *cw, cidx, BATCH, 1, SEQ, 512)
            xs, st_s = _conv_module(xs, hist_s, gm, *cw, cidx, DEC_BATCH, DEC_BATCH, DEC_SEQ, DEC_SEQ)
            conv_p.append(st_p[:, pad:])
            conv_s.append(st_s[:, pad:])
        gx = _vec(xattn_norm[i])
        xp = _xattn_prompt(xp, gx, w_xq, mk_b, mv_b, i, w_xo, 512)
        xs = _xattn_sample(xs, gx, w_xq, cmk, cmv, i, w_xo)
        g = _vec(ffn_norm[i, 1])
        last = gf if i == DEPTH - 1 else None
        xp = _ffn(xp, g, w_in, w_out, i, 1, FFN_TM, final_g=last)
        xs = _ffn(xs, g, w_in, w_out, i, 1, tm_s, final_g=last)

    y_prompt = xp.reshape(BATCH, SEQ, D_MODEL)
    y_sample = xs.reshape(DEC_BATCH, DEC_SEQ, D_MODEL)
    na = n_attn
    kt_all, v_all = kv_p
    mem_shape = (DEPTH, BATCH, N_MEM, N_XHEADS, XHEAD_DIM)
    new_k_prompt = kt_all.reshape(na, BATCH, 2 * N_HEADS, DQK, SEQ).transpose(0, 1, 4, 2, 3)
    return (y_prompt, y_sample,
            new_k_prompt,
            v_all.reshape(na, BATCH, SEQ, N_HEADS, DV),
            jnp.stack(conv_p),
            _from_head_tile_view(mk).reshape(mem_shape),
            _from_head_tile_view(mv).reshape(mem_shape),
            jnp.stack(k_s).reshape(na, DEC_BATCH, DEC_SEQ, 2 * N_HEADS, DQK),
            jnp.stack(v_s).reshape(na, DEC_BATCH, DEC_SEQ, N_HEADS, DV),
            jnp.stack(conv_s))
```

```python
import functools
import math

import jax
import jax.numpy as jnp
from jax import lax
from jax.experimental import pallas as pl
from jax.experimental.pallas import tpu as pltpu

D_MODEL = 1024
BATCH = 8
SEQ = 2048
DEPTH = 4
DEC_BATCH = 32
DEC_SEQ = 8
PAST_LEN = 8192
PAGE_SIZE = 128
N_MIXERS = 2
N_HEADS = 8
DQK = D_MODEL // N_HEADS // 2
DV = 2 * DQK
N_BUCKETS = 32
MAX_DISTANCE = 128
CONV_W = 31
D_FF = 2816
N_MEM = 256
N_XHEADS = 4
XHEAD_DIM = D_MODEL // N_XHEADS
EPS = 1e-6
LOG2E = math.log2(math.e)

F32 = jnp.float32
BF16 = jnp.bfloat16

VMEM_LIMIT = 56 * 1024 * 1024
LANES = 128
SUBLANES = 8
HIST_PAD = 32
FFN_CHUNKS = ((0, 1536), (1536, D_FF))
ROW_TILE = 1024
FFN_TM = 1024
QKV_TM = 512
ATT_TQ = 256
ATT_TK = 256
PAGES_PER_STEP = 16
HT_ROWS = N_XHEADS * (XHEAD_DIM // LANES)
N_PAGES = PAST_LEN // PAGE_SIZE
QROWS = 2 * N_HEADS * DEC_SEQ
HROWS = 2 * DEC_SEQ


def _params(sem):
    return pltpu.CompilerParams(dimension_semantics=sem, vmem_limit_bytes=VMEM_LIMIT)


def _rms(x, g):
    return x * lax.rsqrt(jnp.mean(x * x, axis=-1, keepdims=True) + EPS) * g


def _dot(a, b):
    return jnp.dot(a, b, preferred_element_type=F32)


def _dot_nt(a, b):
    return lax.dot_general(a, b, (((1,), (1,)), ((), ())), preferred_element_type=F32)


def _ffn_kernel(x_ref, g_ref, win_ref, wout_ref, *rest, final):
    x = x_ref[...]
    h = _rms(x, g_ref[...]).astype(BF16)
    acc = None
    for c0, c1 in FFN_CHUNKS:
        a = _dot(h, win_ref[:, c0:c1])
        u = _dot(h, win_ref[:, D_FF + c0:D_FF + c1])
        act = ((a * jax.nn.sigmoid(a)) * u).astype(BF16)
        part = _dot(act, wout_ref[c0:c1, :])
        acc = part if acc is None else acc + part
    y = x + 0.5 * acc
    if final:
        gf_ref, o_ref = rest
        o_ref[...] = _rms(y, gf_ref[...])
    else:
        rest[0][...] = y


def _ffn(x, g, w_in, w_out, layer, half, tm, final_g=None):
    m = x.shape[0]
    row = pl.BlockSpec((tm, D_MODEL), lambda i: (i, 0))
    vec = pl.BlockSpec((1, D_MODEL), lambda i: (0, 0))
    resident = dict(pipeline_mode=pl.Buffered(1))
    in_specs = [row, vec,
                pl.BlockSpec((None, None, D_MODEL, 2 * D_FF), lambda i: (layer, half, 0, 0), **resident),
                pl.BlockSpec((None, None, D_FF, D_MODEL), lambda i: (layer, half, 0, 0), **resident)]
    args = [x, g, w_in, w_out]
    if final_g is not None:
        in_specs.append(vec)
        args.append(final_g)
    return pl.pallas_call(
        functools.partial(_ffn_kernel, final=final_g is not None),
        grid=(m // tm,),
        in_specs=in_specs,
        out_specs=row,
        out_shape=jax.ShapeDtypeStruct((m, D_MODEL), F32),
        compiler_params=_params(("parallel",)),
        name="ffn",
    )(*args)


def _qkv_p_kernel(x_ref, g_ref, wqt_ref, wkt_ref, wv_ref, *rest):
    qt_ref, kt_ref, kb_ref, v_ref, vt_ref = rest[-5:]
    h = _rms(x_ref[...], g_ref[...]).astype(BF16)
    qt_ref[...] = (_dot_nt(wqt_ref[...], h) * (DQK ** -0.5 * LOG2E)).astype(BF16)
    kt = _dot_nt(wkt_ref[...], h)
    kt_ref[...] = kt
    kb_ref[...] = kt.T.astype(BF16)
    v = _dot(h, wv_ref[...])
    vt_ref[...] = v.T.astype(BF16)
    rows = v.shape[0]
    for n in range(N_HEADS):
        v_ref[pl.ds(n, rows, stride=N_HEADS), :] = v[:, n * DV:(n + 1) * DV]


def _qkv_prompt(x, g, wqkv, wqkv_t, a, n_layers, tm, prev=None):
    nt = SEQ // tm
    row = pl.BlockSpec((tm, D_MODEL), lambda b, t: (b * nt + t, 0))
    tr = pl.BlockSpec((None, D_MODEL, tm), lambda b, t: (b, 0, t))

    def wspec(j):
        return pl.BlockSpec((None, D_MODEL, D_MODEL), lambda b, t: (a, 0, j), pipeline_mode=pl.Buffered(1))

    def wtspec(j):
        return pl.BlockSpec((None, D_MODEL, D_MODEL), lambda b, t: (a, j, 0), pipeline_mode=pl.Buffered(1))

    in_specs = [row, pl.BlockSpec((1, D_MODEL), lambda b, t: (0, 0)), wtspec(0), wtspec(1), wspec(2)]
    args = [x, g, wqkv_t, wqkv_t, wqkv]
    aliases = {}
    if prev is not None:
        in_specs += [pl.BlockSpec(memory_space=pl.ANY)] * 2
        aliases = {len(args): 1, len(args) + 1: 3}
        args += list(prev)
    tr_sds = jax.ShapeDtypeStruct((BATCH, D_MODEL, SEQ), BF16)
    return pl.pallas_call(
        _qkv_p_kernel,
        grid=(BATCH, nt),
        in_specs=in_specs,
        out_specs=[tr,
                   pl.BlockSpec((None, None, D_MODEL, tm), lambda b, t: (a, b, 0, t)),
                   row,
                   pl.BlockSpec((None, tm * N_HEADS, DV), lambda b, t: (a, b * nt + t, 0)),
                   tr],
        out_shape=[tr_sds,
                   jax.ShapeDtypeStruct((n_layers, BATCH, D_MODEL, SEQ), F32),
                   jax.ShapeDtypeStruct((BATCH * SEQ, D_MODEL), BF16),
                   jax.ShapeDtypeStruct((n_layers, BATCH * SEQ * N_HEADS, DV), F32),
                   tr_sds],
        input_output_aliases=aliases,
        compiler_params=_params(("parallel", "parallel")),
        name="qkv_prompt",
    )(*args)


def _qkv_s_kernel(x_ref, g_ref, w_ref, q_ref, k_ref, v_ref, h_ref):
    n = pl.program_id(0)

    @pl.when(n == 0)
    def _():
        h_ref[...] = _rms(x_ref[...], g_ref[...]).astype(BF16)

    y = _dot(h_ref[...], w_ref[...])

    @pl.when(n == 0)
    def _():
        q_ref[...] = y * (DQK ** -0.5)

    @pl.when(n == 1)
    def _():
        k_ref[...] = y

    @pl.when(n == 2)
    def _():
        v_ref[...] = y


def _qkv_sample(x, g, wqkv, a):
    m = x.shape[0]
    full = pl.BlockSpec((m, D_MODEL), lambda n: (0, 0))
    out = jax.ShapeDtypeStruct((m, D_MODEL), F32)
    return pl.pallas_call(
        _qkv_s_kernel,
        grid=(3,),
        in_specs=[full, pl.BlockSpec((1, D_MODEL), lambda n: (0, 0)),
                  pl.BlockSpec((None, D_MODEL, D_MODEL), lambda n: (a, 0, n))],
        out_specs=[full, full, full],
        out_shape=[out, out, out],
        scratch_shapes=[pltpu.VMEM((m, D_MODEL), BF16)],
        compiler_params=_params(("arbitrary",)),
        name="qkv_sample",
    )(x, g, wqkv)


def _bias_kernel(rb_ref, bucket_ref, o_ref, *, scale):
    n = pl.program_id(0)
    bucket = bucket_ref[...]
    acc = jnp.zeros(bucket.shape, F32)
    for b in range(N_BUCKETS):
        acc = jnp.where(bucket == b, rb_ref[b, n], acc)
    o_ref[0] = jnp.where(bucket < 0, -jnp.inf, (acc - rb_ref[N_BUCKETS - 1, n]) * scale)


def _bias_tiles(rel_bias, bucket, scale=1.0):
    r, c = bucket.shape
    return pl.pallas_call(
        functools.partial(_bias_kernel, scale=scale),
        grid=(N_HEADS,),
        in_specs=[pl.BlockSpec(memory_space=pltpu.SMEM),
                  pl.BlockSpec((r, c), lambda n: (0, 0))],
        out_specs=pl.BlockSpec((1, r, c), lambda n: (n, 0, 0)),
        out_shape=jax.ShapeDtypeStruct((N_HEADS, r, c), F32),
        compiler_params=_params(("arbitrary",)),
        name="bias_tiles",
    )(rel_bias, bucket)


def _rel_bucket(n):
    max_exact = N_BUCKETS // 2
    nf = jnp.maximum(n, 1).astype(F32)
    large = max_exact + (jnp.log(nf / max_exact) / math.log(MAX_DISTANCE / max_exact)
                         * (N_BUCKETS - max_exact)).astype(jnp.int32)
    large = jnp.minimum(large, N_BUCKETS - 1)
    return jnp.where(n < max_exact, n, large)


def _lambda(lp_ref, lam_init):
    lp = lp_ref[...]
    s01 = jnp.sum(lp[0:1] * lp[1:2], axis=-1, keepdims=True)
    s23 = jnp.sum(lp[2:3] * lp[3:4], axis=-1, keepdims=True)
    return jnp.exp(s01) - jnp.exp(s23) + lam_init


def _subln(o, g_ref, lam_init):
    return _rms(o, g_ref[...]) * (1.0 - lam_init)


def _fold_rows(s, op):
    acc = s[0:SUBLANES]
    for r in range(1, s.shape[0] // SUBLANES):
        acc = op(acc, s[r * SUBLANES:(r + 1) * SUBLANES])
    return acc


def _attn_p_scores(c, qt_ref, k_ref, bias_ref, s_ref):
    tq, tk = ATT_TQ, ATT_TK
    qt = qt_ref[:, c * tq:(c + 1) * tq]
    row = lax.broadcasted_iota(jnp.int32, qt.shape, 0)
    zero = jnp.zeros_like(qt)
    qq = jnp.concatenate([jnp.where(row < DQK, qt, zero), jnp.where(row >= DQK, qt, zero)], axis=1)
    m8 = None
    for j in range(c + 1):
        s = _dot(k_ref[j * tk:(j + 1) * tk, :], qq)
        if c - j < 2:
            s = s + bias_ref[0, c - j]
        s_ref[j] = s
        t = _fold_rows(s, jnp.maximum)
        m8 = t if m8 is None else jnp.maximum(m8, t)
    return jnp.max(m8, axis=0, keepdims=True)


def _attn_p_softmax(c, m, s_ref, p_ref):
    tk = ATT_TK
    l8 = None
    for j in range(c + 1):
        p = jnp.exp2(s_ref[j] - m)
        t = _fold_rows(p, jnp.add)
        l8 = t if l8 is None else l8 + t
        p_ref[j * tk:(j + 1) * tk, :] = p.astype(BF16)
    return jnp.sum(l8, axis=0, keepdims=True)


def _attn_p_out(c, l, lam, vt_ref, g_ref, o_ref, p_ref, lam_init):
    tq = ATT_TQ
    kend = (c + 1) * ATT_TK
    ot = _dot(vt_ref[:, :kend], p_ref[:kend, :]) / l
    o = (ot[:, :tq] - lam * ot[:, tq:]).T
    o_ref[c * tq:(c + 1) * tq, :] = _subln(o, g_ref, lam_init).astype(o_ref.dtype)


def _attn_p_kernel(qt_ref, k_ref, vt_ref, bias_ref, lp_ref, g_ref, o_ref, s_ref, p_ref, *, lam_init):
    lam = _lambda(lp_ref, lam_init)
    nq = SEQ // ATT_TQ
    m = _attn_p_scores(0, qt_ref, k_ref, bias_ref, s_ref.at[0])
    for c in range(nq):
        m_next = None
        if c + 1 < nq:
            m_next = _attn_p_scores(c + 1, qt_ref, k_ref, bias_ref, s_ref.at[(c + 1) % 2])
        l = _attn_p_softmax(c, m, s_ref.at[c % 2], p_ref.at[c % 2])
        _attn_p_out(c, l, lam, vt_ref, g_ref, o_ref, p_ref.at[c % 2], lam_init)
        m = m_next


def _attn_prompt(qt, kb, vt, bias, lp, g, lam_init):
    nk = SEQ // ATT_TK
    return pl.pallas_call(
        functools.partial(_attn_p_kernel, lam_init=lam_init),
        grid=(BATCH, N_HEADS),
        in_specs=[
            pl.BlockSpec((None, DV, SEQ), lambda b, n: (b, n, 0)),
            pl.BlockSpec((SEQ, DV), lambda b, n: (b, n)),
            pl.BlockSpec((None, DV, SEQ), lambda b, n: (b, n, 0)),
            pl.BlockSpec((1, 2, ATT_TK, 2 * ATT_TQ), lambda b, n: (n, 0, 0, 0)),
            pl.BlockSpec((4, DQK), lambda b, n: (0, 0)),
            pl.BlockSpec((1, DV), lambda b, n: (0, 0)),
        ],
        out_specs=pl.BlockSpec((SEQ, DV), lambda b, n: (b, n)),
        out_shape=jax.ShapeDtypeStruct((BATCH * SEQ, D_MODEL), BF16),
        scratch_shapes=[pltpu.VMEM((2, nk, ATT_TK, 2 * ATT_TQ), F32),
                        pltpu.VMEM((2, SEQ, 2 * ATT_TQ), BF16)],
        compiler_params=_params(("parallel", "parallel")),
        name="attn_prompt",
    )(qt, kb, vt, bias, lp, g)


def _attn_s_init(q_ref, qbd_ref, m_ref, l_ref, acc_ref):
    q = q_ref[...]
    qt = jnp.broadcast_to(q[None], (2 * N_HEADS, DEC_SEQ, D_MODEL)).reshape(QROWS, D_MODEL)
    r = lax.broadcasted_iota(jnp.int32, (QROWS, D_MODEL), 0)
    c = lax.broadcasted_iota(jnp.int32, (QROWS, D_MODEL), 1)
    qbd_ref[...] = jnp.where((c >> 6) == (r >> 3), qt, 0.0).astype(BF16)
    m_ref[...] = jnp.full(m_ref.shape, -jnp.inf, F32)
    l_ref[...] = jnp.zeros_like(l_ref)
    acc_ref[...] = jnp.zeros_like(acc_ref)


def _attn_s_update(s, v_heads, m_ref, l_ref, acc_ref):
    m_old = m_ref[...]
    m_new = jnp.maximum(m_old, jnp.max(s, axis=-1, keepdims=True))
    alpha = jnp.exp(m_old - m_new)
    p = jnp.exp(s - m_new)
    l_ref[...] = alpha * l_ref[...] + jnp.sum(p, axis=-1, keepdims=True)
    pb = p.astype(BF16)
    pv = jnp.concatenate([_dot(pb[n * HROWS:(n + 1) * HROWS], v_heads[n]) for n in range(N_HEADS)], axis=0)
    acc_ref[...] = alpha * acc_ref[...] + pv
    m_ref[...] = m_new


def _attn_s_pages(qbd, k_refs, v_refs):
    ktb = jnp.concatenate([r[0, 0].astype(BF16) for r in k_refs], axis=1)
    s = _dot(qbd, ktb)
    v_heads = [jnp.concatenate([r[0, 0, pl.ds(n, PAGE_SIZE, stride=N_HEADS), :].astype(BF16)
                                for r in v_refs], axis=0) for n in range(N_HEADS)]
    return s, v_heads


def _attn_s_finish(qbd, knew_ref, vnew_ref, bias_ref, lp_ref, g_ref, o_ref, m_ref, l_ref, acc_ref, lam_init):
    pad = jnp.zeros((PAGE_SIZE - DEC_SEQ, D_MODEL), F32)
    kn = jnp.concatenate([knew_ref[...], pad], axis=0).astype(BF16)
    vn = jnp.concatenate([vnew_ref[...], pad], axis=0).astype(BF16)
    _attn_s_update(_dot_nt(qbd, kn) + bias_ref[:, PAGE_SIZE:],
                   [vn[:, n * DV:(n + 1) * DV] for n in range(N_HEADS)], m_ref, l_ref, acc_ref)
    o_all = acc_ref[...] / l_ref[...]
    lam = _lambda(lp_ref, lam_init)
    for n in range(N_HEADS):
        o0 = o_all[n * HROWS:n * HROWS + DEC_SEQ]
        o1 = o_all[n * HROWS + DEC_SEQ:(n + 1) * HROWS]
        o_ref[:, n * DV:(n + 1) * DV] = _subln(o0 - lam * o1, g_ref, lam_init)


def _attn_s_kernel(pt_ref, q_ref, *refs, lam_init):
    del pt_ref
    npg = PAGES_PER_STEP
    k_refs = refs[:npg]
    v_refs = refs[npg:2 * npg]
    (knew_ref, vnew_ref, bias_ref, lp_ref, g_ref, o_ref,
     qbd_ref, m_ref, l_ref, acc_ref) = refs[2 * npg:]
    g = pl.program_id(1)
    ng = pl.num_programs(1)

    pl.when(g == 0)(functools.partial(_attn_s_init, q_ref, qbd_ref, m_ref, l_ref, acc_ref))
    qbd = qbd_ref[...]
    s, v_heads = _attn_s_pages(qbd, k_refs, v_refs)
    last = jnp.where(g == ng - 1, bias_ref[:, :PAGE_SIZE], 0.0)
    s = jnp.concatenate([s[:, :(npg - 1) * PAGE_SIZE], s[:, (npg - 1) * PAGE_SIZE:] + last], axis=1)
    _attn_s_update(s, v_heads, m_ref, l_ref, acc_ref)
    pl.when(g == ng - 1)(functools.partial(_attn_s_finish, qbd, knew_ref, vnew_ref, bias_ref, lp_ref, g_ref,
                                           o_ref, m_ref, l_ref, acc_ref, lam_init))


def _attn_sample(page_table, q, cache_kt, cache_v, layer, knew, vnew, bias, lp, g, lam_init):
    npg = PAGES_PER_STEP
    ng = N_PAGES // npg

    def page_spec(i, rows):
        return pl.BlockSpec((1, 1, rows, LANES),
                            lambda b, gg, pt: (layer, pt[b * N_PAGES + gg * npg + i], 0, 0))

    row8 = pl.BlockSpec((DEC_SEQ, D_MODEL), lambda b, gg, pt: (b, 0))
    grid_spec = pltpu.PrefetchScalarGridSpec(
        num_scalar_prefetch=1,
        grid=(DEC_BATCH, ng),
        in_specs=([row8] + [page_spec(i, D_MODEL) for i in range(npg)]
                  + [page_spec(i, PAGE_SIZE * N_HEADS) for i in range(npg)] + [
            row8, row8,
            pl.BlockSpec((QROWS, 2 * PAGE_SIZE), lambda b, gg, pt: (0, 0)),
            pl.BlockSpec((4, DQK), lambda b, gg, pt: (0, 0)),
            pl.BlockSpec((1, DV), lambda b, gg, pt: (0, 0)),
        ]),
        out_specs=row8,
        scratch_shapes=[pltpu.VMEM((QROWS, D_MODEL), BF16),
                        pltpu.VMEM((QROWS, 1), F32),
                        pltpu.VMEM((QROWS, 1), F32),
                        pltpu.VMEM((QROWS, DV), F32)],
    )
    return pl.pallas_call(
        functools.partial(_attn_s_kernel, lam_init=lam_init),
        grid_spec=grid_spec,
        out_shape=jax.ShapeDtypeStruct((DEC_BATCH * DEC_SEQ, D_MODEL), F32),
        compiler_params=_params(("parallel", "arbitrary")),
        name="attn_sample",
    )(page_table, q, *([cache_kt] * npg), *([cache_v] * npg), knew, vnew, bias, lp, g)


def _proj_res_kernel(a_ref, w_ref, x_ref, o_ref):
    o_ref[...] = x_ref[...] + _dot(a_ref[...].astype(BF16), w_ref[...])


def _proj_res(a, w, layer, x, tm):
    m = x.shape[0]
    row = pl.BlockSpec((tm, D_MODEL), lambda i: (i, 0))
    return pl.pallas_call(
        _proj_res_kernel,
        grid=(m // tm,),
        in_specs=[row, pl.BlockSpec((None, D_MODEL, D_MODEL), lambda i: (layer, 0, 0)), row],
        out_specs=row,
        out_shape=jax.ShapeDtypeStruct((m, D_MODEL), F32),
        compiler_params=_params(("parallel",)),
        name="proj_res",
    )(a, w, x)


def _conv_kernel(x_ref, hist_ref, g_ref, w1_ref, b1_ref, wdw_ref, bdw_ref, lng_ref, lnb_ref,
                 w2_ref, b2_ref, o_ref, st_ref, buf_ref, *win_refs, nbb, tt):
    t = pl.program_id(1)

    @pl.when(t == 0)
    def _():
        buf_ref[:, 0:HIST_PAD, :] = hist_ref[...]

    @pl.when(t > 0)
    def _():
        buf_ref[:, 0:HIST_PAD, :] = buf_ref[:, tt:tt + HIST_PAD, :]

    x = x_ref[...]
    h = _rms(x, g_ref[...]).astype(BF16)
    ag = _dot(h, w1_ref[...]) + b1_ref[...]
    u = ag[:, :D_MODEL] * jax.nn.sigmoid(ag[:, D_MODEL:])
    buf_ref[:, HIST_PAD:, :] = u.reshape(nbb, tt, D_MODEL)

    off = HIST_PAD - (CONV_W - 1)
    y = None
    for b in range(SUBLANES):
        es = [e for e in range(off, off + CONV_W) if e % SUBLANES == b]
        if b == 0:
            win_ref = buf_ref
        else:
            win_ref = win_refs[b % 2]
            win_ref[...] = buf_ref[:, b:b + win_ref.shape[1], :]
        for e in es:
            term = win_ref[:, e - b:e - b + tt, :] * wdw_ref[e - off:e - off + 1, :]
            y = term if y is None else y + term
    y = y.reshape(nbb * tt, D_MODEL) + bdw_ref[...]
    mu = jnp.mean(y, axis=-1, keepdims=True)
    yc = y - mu
    z = yc * lax.rsqrt(jnp.mean(yc * yc, axis=-1, keepdims=True) + EPS) * lng_ref[...] + lnb_ref[...]
    z = z * jax.nn.sigmoid(z)
    o_ref[...] = x + _dot(z.astype(BF16), w2_ref[...]) + b2_ref[...]
    st_ref[...] = buf_ref[:, tt:tt + HIST_PAD, :]


def _conv_module(x, hist, g, w1, b1, wdw, bdw, lng, lnb, w2, b2, cidx, nb, nbb, t_len, tt):
    nt = t_len // tt
    row = pl.BlockSpec((nbb * tt, D_MODEL), lambda b, t: (b * nt + t, 0))
    vec = pl.BlockSpec((1, D_MODEL), lambda b, t: (0, 0))
    st = pl.BlockSpec((nbb, HIST_PAD, D_MODEL), lambda b, t: (b, 0, 0))
    return pl.pallas_call(
        functools.partial(_conv_kernel, nbb=nbb, tt=tt),
        grid=(nb // nbb, nt),
        in_specs=[row, st, vec,
                  pl.BlockSpec((None, D_MODEL, 2 * D_MODEL), lambda b, t: (cidx, 0, 0)),
                  pl.BlockSpec((1, 2 * D_MODEL), lambda b, t: (0, 0)),
                  pl.BlockSpec((HIST_PAD, D_MODEL), lambda b, t: (0, 0)),
                  vec, vec, vec,
                  pl.BlockSpec((None, D_MODEL, D_MODEL), lambda b, t: (cidx, 0, 0)),
                  vec],
        out_specs=[row, st],
        out_shape=[jax.ShapeDtypeStruct((nb * t_len, D_MODEL), F32),
                   jax.ShapeDtypeStruct((nb, HIST_PAD, D_MODEL), F32)],
        scratch_shapes=[pltpu.VMEM((nbb, HIST_PAD + tt, D_MODEL), F32)]
                       + [pltpu.VMEM((nbb, HIST_PAD - SUBLANES + tt, D_MODEL), F32)] * 2,
        compiler_params=_params(("parallel", "arbitrary")),
        name="conv_module",
    )(x, hist, g, w1, b1, wdw, bdw, lng, lnb, w2, b2)


def _head_tile_view(a):
    lead, m = a.shape[:-3], a.shape[-3]
    a = a.reshape(lead + (m, N_XHEADS, XHEAD_DIM // LANES, LANES))
    return jnp.swapaxes(a, -3, -2).reshape(lead + (m * HT_ROWS, LANES))


def _from_head_tile_view(x):
    lead, m = x.shape[:-2], x.shape[-2] // HT_ROWS
    x = x.reshape(lead + (m, XHEAD_DIM // LANES, N_XHEADS, LANES))
    return jnp.swapaxes(x, -3, -2).reshape(lead + (m, N_XHEADS, XHEAD_DIM))


def _memkv_kernel(m_ref, w_ref, o_ref, ob_ref):
    y = _dot(m_ref[...].astype(BF16), w_ref[...])
    ob_ref[...] = y.astype(BF16)
    rows = y.shape[0]
    for hd in range(N_XHEADS):
        for c in range(XHEAD_DIM // LANES):
            col = hd * XHEAD_DIM + c * LANES
            o_ref[pl.ds(c * N_XHEADS + hd, rows, stride=HT_ROWS), :] = y[:, col:col + LANES]


def _mem_kv(mem, w_kv, j, tm):
    m = mem.shape[0]
    return pl.pallas_call(
        _memkv_kernel,
        grid=(DEPTH, m // tm),
        in_specs=[pl.BlockSpec((tm, D_MODEL), lambda i, r: (r, 0)),
                  pl.BlockSpec((None, D_MODEL, D_MODEL), lambda i, r: (i, 0, j))],
        out_specs=[pl.BlockSpec((None, tm * HT_ROWS, LANES), lambda i, r: (i, r, 0)),
                   pl.BlockSpec((None, tm, D_MODEL), lambda i, r: (i, r, 0))],
        out_shape=[jax.ShapeDtypeStruct((DEPTH, m * HT_ROWS, LANES), F32),
                   jax.ShapeDtypeStruct((DEPTH, m, D_MODEL), BF16)],
        compiler_params=_params(("parallel", "parallel")),
        name="mem_kv",
    )(mem, w_kv)


def _xhead(q, mk, mv):
    s = _dot_nt(q, mk)
    p = jnp.exp(s - jnp.max(s, axis=-1, keepdims=True))
    l = jnp.sum(p, axis=-1, keepdims=True)
    return _dot(p.astype(BF16), mv) / l


def _xattn_p_kernel(x_ref, g_ref, wq_ref, mk_ref, mv_ref, wo_ref, o_ref):
    x = x_ref[...]
    h = _rms(x, g_ref[...]).astype(BF16)
    q = (_dot(h, wq_ref[...]) * (XHEAD_DIM ** -0.5)).astype(BF16)
    outs = []
    for hd in range(N_XHEADS):
        cs = slice(hd * XHEAD_DIM, (hd + 1) * XHEAD_DIM)
        outs.append(_xhead(q[:, cs], mk_ref[:, cs], mv_ref[:, cs]))
    o = jnp.concatenate(outs, axis=-1)
    o_ref[...] = x + _dot(o.astype(BF16), wo_ref[...])


def _xattn_prompt(x, g, wq, mk_b, mv_b, layer, wo, tm):
    m = x.shape[0]
    per_b = SEQ // tm
    row = pl.BlockSpec((tm, D_MODEL), lambda i: (i, 0))
    wspec = pl.BlockSpec((None, D_MODEL, D_MODEL), lambda i: (layer, 0, 0))
    mem = pl.BlockSpec((None, N_MEM, D_MODEL), lambda i: (layer, i // per_b, 0))
    return pl.pallas_call(
        _xattn_p_kernel,
        grid=(m // tm,),
        in_specs=[row, pl.BlockSpec((1, D_MODEL), lambda i: (0, 0)), wspec, mem, mem, wspec],
        out_specs=row,
        out_shape=jax.ShapeDtypeStruct((m, D_MODEL), F32),
        compiler_params=_params(("parallel",)),
        name="xattn_prompt",
    )(x, g, wq, mk_b, mv_b, wo)


def _xattn_s_kernel(x_ref, g_ref, wq_ref, mk_ref, mv_ref, wo_ref, o_ref, q_ref, a_ref):
    b = pl.program_id(0)

    @pl.when(b == 0)
    def _():
        h = _rms(x_ref[...], g_ref[...]).astype(BF16)
        q_ref[...] = _dot(h, wq_ref[...]) * (XHEAD_DIM ** -0.5)

    r0 = pl.multiple_of(b * DEC_SEQ, DEC_SEQ)
    q = q_ref[pl.ds(r0, DEC_SEQ), :].astype(BF16)
    def head(ref, hd):
        return jnp.concatenate([ref[pl.ds(c * N_XHEADS + hd, N_MEM, stride=HT_ROWS), :]
                                for c in range(XHEAD_DIM // LANES)], axis=1).astype(BF16)

    outs = []
    for hd in range(N_XHEADS):
        outs.append(_xhead(q[:, hd * XHEAD_DIM:(hd + 1) * XHEAD_DIM], head(mk_ref, hd), head(mv_ref, hd)))
    a_ref[pl.ds(r0, DEC_SEQ), :] = jnp.concatenate(outs, axis=-1)

    @pl.when(b == pl.num_programs(0) - 1)
    def _():
        o_ref[...] = x_ref[...] + _dot(a_ref[...].astype(BF16), wo_ref[...])


def _xattn_sample(x, g, wq, mem_k, mem_v, layer, wo):
    m = x.shape[0]
    full = pl.BlockSpec((m, D_MODEL), lambda b: (0, 0))
    wspec = pl.BlockSpec((None, D_MODEL, D_MODEL), lambda b: (layer, 0, 0))
    mem = pl.BlockSpec((None, None, N_MEM * HT_ROWS, LANES), lambda b: (layer, b, 0, 0))
    return pl.pallas_call(
        _xattn_s_kernel,
        grid=(DEC_BATCH,),
        in_specs=[full, pl.BlockSpec((1, D_MODEL), lambda b: (0, 0)), wspec, mem, mem, wspec],
        out_specs=full,
        out_shape=jax.ShapeDtypeStruct((m, D_MODEL), F32),
        scratch_shapes=[pltpu.VMEM((m, D_MODEL), F32), pltpu.VMEM((m, D_MODEL), F32)],
        compiler_params=_params(("arbitrary",)),
        name="xattn_sample",
    )(x, g, wq, mem_k, mem_v, wo)


def _vec(v):
    return v.reshape(1, -1).astype(F32)


def kernel(x_prompt, x_sample, cache_k_diff, cache_v_diff, state_conv, cache_mem_k, cache_mem_v, page_table, mem_prompt, ffn_norm, ffn_w_in, ffn_w_out, mix_norm, diff_w_qkv, diff_lambda, diff_subln, diff_w_o, rel_bias, conv_w_pw1, conv_b_pw1, conv_w_dw, conv_b_dw, conv_ln_g, conv_ln_b, conv_w_pw2, conv_b_pw2, xattn_norm, xattn_w_q, xattn_w_kv, xattn_w_o, final_norm):
    mp = BATCH * SEQ
    ms = DEC_BATCH * DEC_SEQ
    xp = x_prompt.reshape(mp, D_MODEL)
    xs = x_sample.reshape(ms, D_MODEL)
    tm_p, tm_s = ROW_TILE, ms

    w_in = ffn_w_in.astype(BF16)
    w_out = ffn_w_out.astype(BF16)
    wqkv = diff_w_qkv.astype(BF16)
    wqkv_t = jnp.swapaxes(diff_w_qkv, 1, 2).astype(BF16)
    w_do = diff_w_o.astype(BF16)
    w_pw1 = conv_w_pw1.astype(BF16)
    w_pw2 = conv_w_pw2.astype(BF16)
    w_xq = xattn_w_q.astype(BF16)
    w_xkv = xattn_w_kv.astype(BF16)
    w_xo = xattn_w_o.astype(BF16)

    kk = jnp.arange(ATT_TK)[:, None]
    qq = jnp.arange(ATT_TQ)[None, :]
    d0 = qq - kk
    bucket_p = jnp.concatenate([jnp.where(d0 >= 0, _rel_bucket(jnp.maximum(d0, 0)), -1),
                                _rel_bucket(d0 + ATT_TK)], axis=0).astype(jnp.int32)
    bucket_p = jnp.concatenate([bucket_p, bucket_p], axis=1)
    bias_p = _bias_tiles(rel_bias, bucket_p, LOG2E).reshape(N_HEADS, 2, ATT_TK, 2 * ATT_TQ)
    rs = jnp.arange(DEC_SEQ)[:, None]
    cs = jnp.arange(PAGE_SIZE)[None, :]
    d_new = rs - cs
    bucket_s = jnp.concatenate(
        [_rel_bucket(PAGE_SIZE + rs - cs),
         jnp.where((d_new >= 0) & (cs < DEC_SEQ), _rel_bucket(jnp.maximum(d_new, 0)), -1)],
        axis=1).astype(jnp.int32)
    bias_s = _bias_tiles(rel_bias, bucket_s)
    bias_s = jnp.broadcast_to(bias_s[:, None], (N_HEADS, 2, DEC_SEQ, 2 * PAGE_SIZE)).reshape(
        QROWS, 2 * PAGE_SIZE)

    n_pool = cache_k_diff.shape[1]
    ckt = cache_k_diff.transpose(0, 1, 3, 4, 2).reshape(-1, n_pool, D_MODEL, PAGE_SIZE)
    cv = cache_v_diff.reshape(-1, n_pool, PAGE_SIZE * N_HEADS, DV)
    pt = page_table.reshape(-1).astype(jnp.int32)

    mem2d = mem_prompt.reshape(BATCH * N_MEM, D_MODEL)
    mk, mk_b = _mem_kv(mem2d, w_xkv, 0, 512)
    mv, mv_b = _mem_kv(mem2d, w_xkv, 1, 512)
    cmk = _head_tile_view(cache_mem_k)
    cmv = _head_tile_view(cache_mem_v)

    n_attn = (DEPTH + N_MIXERS - 1) // N_MIXERS
    gf = _vec(final_norm)
    kv_p = None
    conv_p, k_s, v_s, conv_s = [], [], [], []
    for i in range(DEPTH):
        g = _vec(ffn_norm[i, 0])
        xs = _ffn(xs, g, w_in, w_out, i, 0, tm_s)
        gm = _vec(mix_norm[i])
        if i % N_MIXERS == 0:
            a = i // N_MIXERS
            lam_init = 0.8 - 0.6 * math.exp(-0.3 * i)
            lp = diff_lambda[a].astype(F32)
            gs = _vec(diff_subln[a])
            q, k, v = _qkv_sample(xs, gm, wqkv, a)
            k_s.append(k)
            v_s.append(v)
            os_ = _attn_sample(pt, q, ckt, cv, a, k, v, bias_s, lp, gs, lam_init)
            xs = _proj_res(os_, w_do, a, xs, tm_s)
            xp = _ffn(xp, g, w_in, w_out, i, 0, FFN_TM)
            qt, kt_all, kb, v_all, vt = _qkv_prompt(xp, gm, wqkv, wqkv_t, a, n_attn, QKV_TM, prev=kv_p)
            kv_p = (kt_all, v_all)
            op = _attn_prompt(qt, kb, vt, bias_p, lp, gs, lam_init)
            xp = _proj_res(op, w_do, a, xp, tm_p)
        else:
            xp = _ffn(xp, g, w_in, w_out, i, 0, FFN_TM)
            cidx = i // N_MIXERS
            cw = (w_pw1, _vec(conv_b_pw1[cidx]),
                  jnp.pad(conv_w_dw[cidx], ((0, HIST_PAD - CONV_W), (0, 0))), _vec(conv_b_dw[cidx]),
                  _vec(conv_ln_g[cidx]), _vec(conv_ln_b[cidx]),
                  w_pw2, _vec(conv_b_pw2[cidx]))
            pad = HIST_PAD - (CONV_W - 1)
            hist_p = jnp.zeros((BATCH, HIST_PAD, D_MODEL), F32)
            hist_s = jnp.pad(state_conv[cidx], ((0, 0), (pad, 0), (0, 0)))
            xp, st_p = _conv_module(xp, hist_p, gm, *cw, cidx, BATCH, 1, SEQ, 512)
            xs, st_s = _conv_module(xs, hist_s, gm, *cw, cidx, DEC_BATCH, DEC_BATCH, DEC_SEQ, DEC_SEQ)
            conv_p.append(st_p[:, pad:])
            conv_s.append(st_s[:, pad:])
        gx = _vec(xattn_norm[i])
        xp = _xattn_prompt(xp, gx, w_xq, mk_b, mv_b, i, w_xo, tm_p)
        xs = _xattn_sample(xs, gx, w_xq, cmk, cmv, i, w_xo)
        g = _vec(ffn_norm[i, 1])
        last = gf if i == DEPTH - 1 else None
        xp = _ffn(xp, g, w_in, w_out, i, 1, FFN_TM, final_g=last)
        xs = _ffn(xs, g, w_in, w_out, i, 1, tm_s, final_g=last)

    y_prompt = xp.reshape(BATCH, SEQ, D_MODEL)
    y_sample = xs.reshape(DEC_BATCH, DEC_SEQ, D_MODEL)
    na = n_attn
    kt_all, v_all = kv_p
    mem_shape = (DEPTH, BATCH, N_MEM, N_XHEADS, XHEAD_DIM)
    new_k_prompt = kt_all.reshape(na, BATCH, 2 * N_HEADS, DQK, SEQ).transpose(0, 1, 4, 2, 3)
    return (y_prompt, y_sample,
            new_k_prompt,
            v_all.reshape(na, BATCH, SEQ, N_HEADS, DV),
            jnp.stack(conv_p),
            _from_head_tile_view(mk).reshape(mem_shape),
            _from_head_tile_view(mv).reshape(mem_shape),
            jnp.stack(k_s).reshape(na, DEC_BATCH, DEC_SEQ, 2 * N_HEADS, DQK),
            jnp.stack(v_s).reshape(na, DEC_BATCH, DEC_SEQ, N_HEADS, DV),
            jnp.stack(conv_s))
```

```python
import functools
import math

import jax
import jax.numpy as jnp
from jax import lax
from jax.experimental import pallas as pl
from jax.experimental.pallas import tpu as pltpu

D_MODEL = 1024
BATCH = 8
SEQ = 2048
DEPTH = 4
DEC_BATCH = 32
DEC_SEQ = 8
PAST_LEN = 8192
PAGE_SIZE = 128
N_MIXERS = 2
N_HEADS = 8
DQK = D_MODEL // N_HEADS // 2
DV = 2 * DQK
N_BUCKETS = 32
MAX_DISTANCE = 128
CONV_W = 31
D_FF = 2816
N_MEM = 256
N_XHEADS = 4
XHEAD_DIM = D_MODEL // N_XHEADS
EPS = 1e-6
LOG2E = math.log2(math.e)

F32 = jnp.float32
BF16 = jnp.bfloat16

VMEM_LIMIT = 56 * 1024 * 1024
LANES = 128
SUBLANES = 8
HIST_PAD = 32
FFN_CHUNKS = ((0, 1536), (1536, D_FF))
ROW_TILE = 1024
FFN_TM = 1024
QKV_TM = 512
ATT_TQ = 256
ATT_TK = 256
PAGES_PER_STEP = 16
HT_ROWS = N_XHEADS * (XHEAD_DIM // LANES)
N_PAGES = PAST_LEN // PAGE_SIZE
QROWS = 2 * N_HEADS * DEC_SEQ
HROWS = 2 * DEC_SEQ


def _params(sem):
    return pltpu.CompilerParams(dimension_semantics=sem, vmem_limit_bytes=VMEM_LIMIT)


def _rms(x, g):
    return x * lax.rsqrt(jnp.mean(x * x, axis=-1, keepdims=True) + EPS) * g


def _dot(a, b):
    return jnp.dot(a, b, preferred_element_type=F32)


def _dot_nt(a, b):
    return lax.dot_general(a, b, (((1,), (1,)), ((), ())), preferred_element_type=F32)


def _ffn_kernel(x_ref, g_ref, win_ref, wout_ref, *rest, final):
    x = x_ref[...]
    h = _rms(x, g_ref[...]).astype(BF16)
    acc = None
    for c0, c1 in FFN_CHUNKS:
        a = _dot(h, win_ref[:, c0:c1])
        u = _dot(h, win_ref[:, D_FF + c0:D_FF + c1])
        act = ((a * jax.nn.sigmoid(a)) * u).astype(BF16)
        part = _dot(act, wout_ref[c0:c1, :])
        acc = part if acc is None else acc + part
    y = x + 0.5 * acc
    if final:
        gf_ref, o_ref = rest
        o_ref[...] = _rms(y, gf_ref[...])
    else:
        rest[0][...] = y


def _ffn(x, g, w_in, w_out, layer, half, tm, final_g=None):
    m = x.shape[0]
    row = pl.BlockSpec((tm, D_MODEL), lambda i: (i, 0))
    vec = pl.BlockSpec((1, D_MODEL), lambda i: (0, 0))
    resident = dict(pipeline_mode=pl.Buffered(1))
    in_specs = [row, vec,
                pl.BlockSpec((None, None, D_MODEL, 2 * D_FF), lambda i: (layer, half, 0, 0), **resident),
                pl.BlockSpec((None, None, D_FF, D_MODEL), lambda i: (layer, half, 0, 0), **resident)]
    args = [x, g, w_in, w_out]
    if final_g is not None:
        in_specs.append(vec)
        args.append(final_g)
    return pl.pallas_call(
        functools.partial(_ffn_kernel, final=final_g is not None),
        grid=(m // tm,),
        in_specs=in_specs,
        out_specs=row,
        out_shape=jax.ShapeDtypeStruct((m, D_MODEL), F32),
        compiler_params=_params(("parallel",)),
        name="ffn",
    )(*args)


def _qkv_p_kernel(x_ref, g_ref, wqt_ref, wkt_ref, wv_ref, *rest):
    qt_ref, kt_ref, kb_ref, v_ref, vt_ref = rest[-5:]
    h = _rms(x_ref[...], g_ref[...]).astype(BF16)
    qt_ref[...] = (_dot_nt(wqt_ref[...], h) * (DQK ** -0.5 * LOG2E)).astype(BF16)
    kt = _dot_nt(wkt_ref[...], h)
    kt_ref[...] = kt
    kb_ref[...] = kt.T.astype(BF16)
    v = _dot(h, wv_ref[...])
    vt_ref[...] = v.T.astype(BF16)
    rows = v.shape[0]
    for n in range(N_HEADS):
        v_ref[pl.ds(n, rows, stride=N_HEADS), :] = v[:, n * DV:(n + 1) * DV]


def _qkv_prompt(x, g, wqkv, wqkv_t, a, n_layers, tm, prev=None):
    nt = SEQ // tm
    row = pl.BlockSpec((tm, D_MODEL), lambda b, t: (b * nt + t, 0))
    tr = pl.BlockSpec((None, D_MODEL, tm), lambda b, t: (b, 0, t))

    def wspec(j):
        return pl.BlockSpec((None, D_MODEL, D_MODEL), lambda b, t: (a, 0, j), pipeline_mode=pl.Buffered(1))

    def wtspec(j):
        return pl.BlockSpec((None, D_MODEL, D_MODEL), lambda b, t: (a, j, 0), pipeline_mode=pl.Buffered(1))

    in_specs = [row, pl.BlockSpec((1, D_MODEL), lambda b, t: (0, 0)), wtspec(0), wtspec(1), wspec(2)]
    args = [x, g, wqkv_t, wqkv_t, wqkv]
    aliases = {}
    if prev is not None:
        in_specs += [pl.BlockSpec(memory_space=pl.ANY)] * 2
        aliases = {len(args): 1, len(args) + 1: 3}
        args += list(prev)
    tr_sds = jax.ShapeDtypeStruct((BATCH, D_MODEL, SEQ), BF16)
    return pl.pallas_call(
        _qkv_p_kernel,
        grid=(BATCH, nt),
        in_specs=in_specs,
        out_specs=[tr,
                   pl.BlockSpec((None, None, D_MODEL, tm), lambda b, t: (a, b, 0, t)),
                   row,
                   pl.BlockSpec((None, tm * N_HEADS, DV), lambda b, t: (a, b * nt + t, 0)),
                   tr],
        out_shape=[tr_sds,
                   jax.ShapeDtypeStruct((n_layers, BATCH, D_MODEL, SEQ), F32),
                   jax.ShapeDtypeStruct((BATCH * SEQ, D_MODEL), BF16),
                   jax.ShapeDtypeStruct((n_layers, BATCH * SEQ * N_HEADS, DV), F32),
                   tr_sds],
        input_output_aliases=aliases,
        compiler_params=_params(("parallel", "parallel")),
        name="qkv_prompt",
    )(*args)


def _qkv_s_kernel(x_ref, g_ref, w_ref, q_ref, k_ref, v_ref, h_ref):
    n = pl.program_id(0)

    @pl.when(n == 0)
    def _():
        h_ref[...] = _rms(x_ref[...], g_ref[...]).astype(BF16)

    y = _dot(h_ref[...], w_ref[...])

    @pl.when(n == 0)
    def _():
        q_ref[...] = y * (DQK ** -0.5)

    @pl.when(n == 1)
    def _():
        k_ref[...] = y

    @pl.when(n == 2)
    def _():
        v_ref[...] = y


def _qkv_sample(x, g, wqkv, a):
    m = x.shape[0]
    full = pl.BlockSpec((m, D_MODEL), lambda n: (0, 0))
    out = jax.ShapeDtypeStruct((m, D_MODEL), F32)
    return pl.pallas_call(
        _qkv_s_kernel,
        grid=(3,),
        in_specs=[full, pl.BlockSpec((1, D_MODEL), lambda n: (0, 0)),
                  pl.BlockSpec((None, D_MODEL, D_MODEL), lambda n: (a, 0, n))],
        out_specs=[full, full, full],
        out_shape=[out, out, out],
        scratch_shapes=[pltpu.VMEM((m, D_MODEL), BF16)],
        compiler_params=_params(("arbitrary",)),
        name="qkv_sample",
    )(x, g, wqkv)


def _bias_kernel(rb_ref, bucket_ref, o_ref, *, scale):
    n = pl.program_id(0)
    bucket = bucket_ref[...]
    acc = jnp.zeros(bucket.shape, F32)
    for b in range(N_BUCKETS):
        acc = jnp.where(bucket == b, rb_ref[b, n], acc)
    o_ref[0] = jnp.where(bucket < 0, -jnp.inf, (acc - rb_ref[N_BUCKETS - 1, n]) * scale)


def _bias_tiles(rel_bias, bucket, scale=1.0):
    r, c = bucket.shape
    return pl.pallas_call(
        functools.partial(_bias_kernel, scale=scale),
        grid=(N_HEADS,),
        in_specs=[pl.BlockSpec(memory_space=pltpu.SMEM),
                  pl.BlockSpec((r, c), lambda n: (0, 0))],
        out_specs=pl.BlockSpec((1, r, c), lambda n: (n, 0, 0)),
        out_shape=jax.ShapeDtypeStruct((N_HEADS, r, c), F32),
        compiler_params=_params(("arbitrary",)),
        name="bias_tiles",
    )(rel_bias, bucket)


def _rel_bucket(n):
    max_exact = N_BUCKETS // 2
    nf = jnp.maximum(n, 1).astype(F32)
    large = max_exact + (jnp.log(nf / max_exact) / math.log(MAX_DISTANCE / max_exact)
                         * (N_BUCKETS - max_exact)).astype(jnp.int32)
    large = jnp.minimum(large, N_BUCKETS - 1)
    return jnp.where(n < max_exact, n, large)


def _lambda(lp_ref, lam_init):
    lp = lp_ref[...]
    s01 = jnp.sum(lp[0:1] * lp[1:2], axis=-1, keepdims=True)
    s23 = jnp.sum(lp[2:3] * lp[3:4], axis=-1, keepdims=True)
    return jnp.exp(s01) - jnp.exp(s23) + lam_init


def _subln(o, g_ref, lam_init):
    return _rms(o, g_ref[...]) * (1.0 - lam_init)


def _fold_rows(s, op):
    acc = s[0:SUBLANES]
    for r in range(1, s.shape[0] // SUBLANES):
        acc = op(acc, s[r * SUBLANES:(r + 1) * SUBLANES])
    return acc


def _attn_p_scores(c, qt_ref, k_ref, bias_ref, s_ref):
    tq, tk = ATT_TQ, ATT_TK
    qt = qt_ref[:, c * tq:(c + 1) * tq]
    row = lax.broadcasted_iota(jnp.int32, qt.shape, 0)
    zero = jnp.zeros_like(qt)
    qq = jnp.concatenate([jnp.where(row < DQK, qt, zero), jnp.where(row >= DQK, qt, zero)], axis=1)
    m8 = None
    for j in range(c + 1):
        s = _dot(k_ref[j * tk:(j + 1) * tk, :], qq)
        if c - j < 2:
            s = s + bias_ref[0, c - j]
        s_ref[j] = s
        t = _fold_rows(s, jnp.maximum)
        m8 = t if m8 is None else jnp.maximum(m8, t)
    return jnp.max(m8, axis=0, keepdims=True)


def _attn_p_softmax(c, m, s_ref, p_ref):
    tk = ATT_TK
    l8 = None
    for j in range(c + 1):
        p = jnp.exp2(s_ref[j] - m)
        t = _fold_rows(p, jnp.add)
        l8 = t if l8 is None else l8 + t
        p_ref[j * tk:(j + 1) * tk, :] = p.astype(BF16)
    return jnp.sum(l8, axis=0, keepdims=True)


def _attn_p_out(c, l, lam, vt_ref, g_ref, o_ref, p_ref, lam_init):
    tq = ATT_TQ
    kend = (c + 1) * ATT_TK
    ot = _dot(vt_ref[:, :kend], p_ref[:kend, :]) / l
    o = (ot[:, :tq] - lam * ot[:, tq:]).T
    o_ref[c * tq:(c + 1) * tq, :] = _subln(o, g_ref, lam_init).astype(o_ref.dtype)


def _attn_p_kernel(qt_ref, k_ref, vt_ref, bias_ref, lp_ref, g_ref, o_ref, s_ref, p_ref, *, lam_init):
    lam = _lambda(lp_ref, lam_init)
    nq = SEQ // ATT_TQ
    m = _attn_p_scores(0, qt_ref, k_ref, bias_ref, s_ref.at[0])
    for c in range(nq):
        m_next = None
        if c + 1 < nq:
            m_next = _attn_p_scores(c + 1, qt_ref, k_ref, bias_ref, s_ref.at[(c + 1) % 2])
        l = _attn_p_softmax(c, m, s_ref.at[c % 2], p_ref.at[c % 2])
        _attn_p_out(c, l, lam, vt_ref, g_ref, o_ref, p_ref.at[c % 2], lam_init)
        m = m_next


def _attn_prompt(qt, kb, vt, bias, lp, g, lam_init):
    nk = SEQ // ATT_TK
    return pl.pallas_call(
        functools.partial(_attn_p_kernel, lam_init=lam_init),
        grid=(BATCH, N_HEADS),
        in_specs=[
            pl.BlockSpec((None, DV, SEQ), lambda b, n: (b, n, 0)),
            pl.BlockSpec((SEQ, DV), lambda b, n: (b, n)),
            pl.BlockSpec((None, DV, SEQ), lambda b, n: (b, n, 0)),
            pl.BlockSpec((1, 2, ATT_TK, 2 * ATT_TQ), lambda b, n: (n, 0, 0, 0)),
            pl.BlockSpec((4, DQK), lambda b, n: (0, 0)),
            pl.BlockSpec((1, DV), lambda b, n: (0, 0)),
        ],
        out_specs=pl.BlockSpec((SEQ, DV), lambda b, n: (b, n)),
        out_shape=jax.ShapeDtypeStruct((BATCH * SEQ, D_MODEL), BF16),
        scratch_shapes=[pltpu.VMEM((2, nk, ATT_TK, 2 * ATT_TQ), F32),
                        pltpu.VMEM((2, SEQ, 2 * ATT_TQ), BF16)],
        compiler_params=_params(("parallel", "parallel")),
        name="attn_prompt",
    )(qt, kb, vt, bias, lp, g)


def _attn_s_init(q_ref, qbd_ref, m_ref, l_ref, acc_ref):
    q = q_ref[...]
    qt = jnp.broadcast_to(q[None], (2 * N_HEADS, DEC_SEQ, D_MODEL)).reshape(QROWS, D_MODEL)
    r = lax.broadcasted_iota(jnp.int32, (QROWS, D_MODEL), 0)
    c = lax.broadcasted_iota(jnp.int32, (QROWS, D_MODEL), 1)
    qbd_ref[...] = jnp.where((c >> 6) == (r >> 3), qt, 0.0).astype(BF16)
    m_ref[...] = jnp.full(m_ref.shape, -jnp.inf, F32)
    l_ref[...] = jnp.zeros_like(l_ref)
    acc_ref[...] = jnp.zeros_like(acc_ref)


def _attn_s_update(s, v_heads, m_ref, l_ref, acc_ref):
    m_old = m_ref[...]
    m_new = jnp.maximum(m_old, jnp.max(s, axis=-1, keepdims=True))
    alpha = jnp.exp(m_old - m_new)
    p = jnp.exp(s - m_new)
    l_ref[...] = alpha * l_ref[...] + jnp.sum(p, axis=-1, keepdims=True)
    pb = p.astype(BF16)
    pv = jnp.concatenate([_dot(pb[n * HROWS:(n + 1) * HROWS], v_heads[n]) for n in range(N_HEADS)], axis=0)
    acc_ref[...] = alpha * acc_ref[...] + pv
    m_ref[...] = m_new


def _attn_s_pages(qbd, k_refs, v_refs):
    ktb = jnp.concatenate([r[0, 0].astype(BF16) for r in k_refs], axis=1)
    s = _dot(qbd, ktb)
    v_heads = [jnp.concatenate([r[0, 0, pl.ds(n, PAGE_SIZE, stride=N_HEADS), :].astype(BF16)
                                for r in v_refs], axis=0) for n in range(N_HEADS)]
    return s, v_heads


def _attn_s_finish(qbd, knew_ref, vnew_ref, bias_ref, lp_ref, g_ref, o_ref, m_ref, l_ref, acc_ref, lam_init):
    pad = jnp.zeros((PAGE_SIZE - DEC_SEQ, D_MODEL), F32)
    kn = jnp.concatenate([knew_ref[...], pad], axis=0).astype(BF16)
    vn = jnp.concatenate([vnew_ref[...], pad], axis=0).astype(BF16)
    _attn_s_update(_dot_nt(qbd, kn) + bias_ref[:, PAGE_SIZE:],
                   [vn[:, n * DV:(n + 1) * DV] for n in range(N_HEADS)], m_ref, l_ref, acc_ref)
    o_all = acc_ref[...] / l_ref[...]
    lam = _lambda(lp_ref, lam_init)
    for n in range(N_HEADS):
        o0 = o_all[n * HROWS:n * HROWS + DEC_SEQ]
        o1 = o_all[n * HROWS + DEC_SEQ:(n + 1) * HROWS]
        o_ref[:, n * DV:(n + 1) * DV] = _subln(o0 - lam * o1, g_ref, lam_init)


def _attn_s_kernel(pt_ref, q_ref, *refs, lam_init):
    del pt_ref
    npg = PAGES_PER_STEP
    k_refs = refs[:npg]
    v_refs = refs[npg:2 * npg]
    (knew_ref, vnew_ref, bias_ref, lp_ref, g_ref, o_ref,
     qbd_ref, m_ref, l_ref, acc_ref) = refs[2 * npg:]
    g = pl.program_id(1)
    ng = pl.num_programs(1)

    pl.when(g == 0)(functools.partial(_attn_s_init, q_ref, qbd_ref, m_ref, l_ref, acc_ref))
    qbd = qbd_ref[...]
    s, v_heads = _attn_s_pages(qbd, k_refs, v_refs)
    last = jnp.where(g == ng - 1, bias_ref[:, :PAGE_SIZE], 0.0)
    s = jnp.concatenate([s[:, :(npg - 1) * PAGE_SIZE], s[:, (npg - 1) * PAGE_SIZE:] + last], axis=1)
    _attn_s_update(s, v_heads, m_ref, l_ref, acc_ref)
    pl.when(g == ng - 1)(functools.partial(_attn_s_finish, qbd, knew_ref, vnew_ref, bias_ref, lp_ref, g_ref,
                                           o_ref, m_ref, l_ref, acc_ref, lam_init))


def _attn_sample(page_table, q, cache_kt, cache_v, layer, knew, vnew, bias, lp, g, lam_init):
    npg = PAGES_PER_STEP
    ng = N_PAGES // npg

    def page_spec(i, rows):
        return pl.BlockSpec((1, 1, rows, LANES),
                            lambda b, gg, pt: (layer, pt[b * N_PAGES + gg * npg + i], 0, 0))

    row8 = pl.BlockSpec((DEC_SEQ, D_MODEL), lambda b, gg, pt: (b, 0))
    grid_spec = pltpu.PrefetchScalarGridSpec(
        num_scalar_prefetch=1,
        grid=(DEC_BATCH, ng),
        in_specs=([row8] + [page_spec(i, D_MODEL) for i in range(npg)]
                  + [page_spec(i, PAGE_SIZE * N_HEADS) for i in range(npg)] + [
            row8, row8,
            pl.BlockSpec((QROWS, 2 * PAGE_SIZE), lambda b, gg, pt: (0, 0)),
            pl.BlockSpec((4, DQK), lambda b, gg, pt: (0, 0)),
            pl.BlockSpec((1, DV), lambda b, gg, pt: (0, 0)),
        ]),
        out_specs=row8,
        scratch_shapes=[pltpu.VMEM((QROWS, D_MODEL), BF16),
                        pltpu.VMEM((QROWS, 1), F32),
                        pltpu.VMEM((QROWS, 1), F32),
                        pltpu.VMEM((QROWS, DV), F32)],
    )
    return pl.pallas_call(
        functools.partial(_attn_s_kernel, lam_init=lam_init),
        grid_spec=grid_spec,
        out_shape=jax.ShapeDtypeStruct((DEC_BATCH * DEC_SEQ, D_MODEL), F32),
        compiler_params=_params(("parallel", "arbitrary")),
        name="attn_sample",
    )(page_table, q, *([cache_kt] * npg), *([cache_v] * npg), knew, vnew, bias, lp, g)


def _proj_res_kernel(a_ref, w_ref, x_ref, o_ref):
    o_ref[...] = x_ref[...] + _dot(a_ref[...].astype(BF16), w_ref[...])


def _proj_res(a, w, layer, x, tm):
    m = x.shape[0]
    row = pl.BlockSpec((tm, D_MODEL), lambda i: (i, 0))
    return pl.pallas_call(
        _proj_res_kernel,
        grid=(m // tm,),
        in_specs=[row, pl.BlockSpec((None, D_MODEL, D_MODEL), lambda i: (layer, 0, 0)), row],
        out_specs=row,
        out_shape=jax.ShapeDtypeStruct((m, D_MODEL), F32),
        compiler_params=_params(("parallel",)),
        name="proj_res",
    )(a, w, x)


def _conv_kernel(x_ref, hist_ref, g_ref, w1_ref, b1_ref, wdw_ref, bdw_ref, lng_ref, lnb_ref,
                 w2_ref, b2_ref, o_ref, st_ref, buf_ref, *win_refs, nbb, tt):
    t = pl.program_id(1)

    @pl.when(t == 0)
    def _():
        buf_ref[:, 0:HIST_PAD, :] = hist_ref[...]

    @pl.when(t > 0)
    def _():
        buf_ref[:, 0:HIST_PAD, :] = buf_ref[:, tt:tt + HIST_PAD, :]

    x = x_ref[...]
    h = _rms(x, g_ref[...]).astype(BF16)
    ag = _dot(h, w1_ref[...]) + b1_ref[...]
    u = ag[:, :D_MODEL] * jax.nn.sigmoid(ag[:, D_MODEL:])
    buf_ref[:, HIST_PAD:, :] = u.reshape(nbb, tt, D_MODEL)

    off = HIST_PAD - (CONV_W - 1)
    y = None
    for b in range(SUBLANES):
        es = [e for e in range(off, off + CONV_W) if e % SUBLANES == b]
        if b == 0:
            win_ref = buf_ref
        else:
            win_ref = win_refs[b % 2]
            win_ref[...] = buf_ref[:, b:b + win_ref.shape[1], :]
        for e in es:
            term = win_ref[:, e - b:e - b + tt, :] * wdw_ref[e - off:e - off + 1, :]
            y = term if y is None else y + term
    y = y.reshape(nbb * tt, D_MODEL) + bdw_ref[...]
    mu = jnp.mean(y, axis=-1, keepdims=True)
    yc = y - mu
    z = yc * lax.rsqrt(jnp.mean(yc * yc, axis=-1, keepdims=True) + EPS) * lng_ref[...] + lnb_ref[...]
    z = z * jax.nn.sigmoid(z)
    o_ref[...] = x + _dot(z.astype(BF16), w2_ref[...]) + b2_ref[...]
    st_ref[...] = buf_ref[:, tt:tt + HIST_PAD, :]


def _conv_module(x, hist, g, w1, b1, wdw, bdw, lng, lnb, w2, b2, cidx, nb, nbb, t_len, tt):
    nt = t_len // tt
    row = pl.BlockSpec((nbb * tt, D_MODEL), lambda b, t: (b * nt + t, 0))
    vec = pl.BlockSpec((1, D_MODEL), lambda b, t: (0, 0))
    st = pl.BlockSpec((nbb, HIST_PAD, D_MODEL), lambda b, t: (b, 0, 0))
    return pl.pallas_call(
        functools.partial(_conv_kernel, nbb=nbb, tt=tt),
        grid=(nb // nbb, nt),
        in_specs=[row, st, vec,
                  pl.BlockSpec((None, D_MODEL, 2 * D_MODEL), lambda b, t: (cidx, 0, 0)),
                  pl.BlockSpec((1, 2 * D_MODEL), lambda b, t: (0, 0)),
                  pl.BlockSpec((HIST_PAD, D_MODEL), lambda b, t: (0, 0)),
                  vec, vec, vec,
                  pl.BlockSpec((None, D_MODEL, D_MODEL), lambda b, t: (cidx, 0, 0)),
                  vec],
        out_specs=[row, st],
        out_shape=[jax.ShapeDtypeStruct((nb * t_len, D_MODEL), F32),
                   jax.ShapeDtypeStruct((nb, HIST_PAD, D_MODEL), F32)],
        scratch_shapes=[pltpu.VMEM((nbb, HIST_PAD + tt, D_MODEL), F32)]
                       + [pltpu.VMEM((nbb, HIST_PAD - SUBLANES + tt, D_MODEL), F32)] * 2,
        compiler_params=_params(("parallel", "arbitrary")),
        name="conv_module",
    )(x, hist, g, w1, b1, wdw, bdw, lng, lnb, w2, b2)


def _head_tile_view(a):
    lead, m = a.shape[:-3], a.shape[-3]
    a = a.reshape(lead + (m, N_XHEADS, XHEAD_DIM // LANES, LANES))
    return jnp.swapaxes(a, -3, -2).reshape(lead + (m * HT_ROWS, LANES))


def _from_head_tile_view(x):
    lead, m = x.shape[:-2], x.shape[-2] // HT_ROWS
    x = x.reshape(lead + (m, XHEAD_DIM // LANES, N_XHEADS, LANES))
    return jnp.swapaxes(x, -3, -2).reshape(lead + (m, N_XHEADS, XHEAD_DIM))


def _memkv_kernel(m_ref, w_ref, o_ref, ob_ref):
    y = _dot(m_ref[...].astype(BF16), w_ref[...])
    ob_ref[...] = y.astype(BF16)
    rows = y.shape[0]
    for hd in range(N_XHEADS):
        for c in range(XHEAD_DIM // LANES):
            col = hd * XHEAD_DIM + c * LANES
            o_ref[pl.ds(c * N_XHEADS + hd, rows, stride=HT_ROWS), :] = y[:, col:col + LANES]


def _mem_kv(mem, w_kv, j, tm):
    m = mem.shape[0]
    return pl.pallas_call(
        _memkv_kernel,
        grid=(DEPTH, m // tm),
        in_specs=[pl.BlockSpec((tm, D_MODEL), lambda i, r: (r, 0)),
                  pl.BlockSpec((None, D_MODEL, D_MODEL), lambda i, r: (i, 0, j))],
        out_specs=[pl.BlockSpec((None, tm * HT_ROWS, LANES), lambda i, r: (i, r, 0)),
                   pl.BlockSpec((None, tm, D_MODEL), lambda i, r: (i, r, 0))],
        out_shape=[jax.ShapeDtypeStruct((DEPTH, m * HT_ROWS, LANES), F32),
                   jax.ShapeDtypeStruct((DEPTH, m, D_MODEL), BF16)],
        compiler_params=_params(("parallel", "parallel")),
        name="mem_kv",
    )(mem, w_kv)


def _xhead(q, mk, mv):
    s = _dot_nt(q, mk)
    p = jnp.exp(s - jnp.max(s, axis=-1, keepdims=True))
    l = jnp.sum(p, axis=-1, keepdims=True)
    return _dot(p.astype(BF16), mv) / l


def _xattn_p_kernel(x_ref, g_ref, wq_ref, mk_ref, mv_ref, wo_ref, *rest):
    o_ref = rest[-1]
    x = x_ref[...]
    if len(rest) == 3:
        a_ref, wa_ref = rest[:2]
        x = x + _dot(a_ref[...], wa_ref[...])
    h = _rms(x, g_ref[...]).astype(BF16)
    q = (_dot(h, wq_ref[...]) * (XHEAD_DIM ** -0.5)).astype(BF16)
    outs = []
    for hd in range(N_XHEADS):
        cs = slice(hd * XHEAD_DIM, (hd + 1) * XHEAD_DIM)
        outs.append(_xhead(q[:, cs], mk_ref[:, cs], mv_ref[:, cs]))
    o = jnp.concatenate(outs, axis=-1)
    o_ref[...] = x + _dot(o.astype(BF16), wo_ref[...])


def _xattn_prompt(x, g, wq, mk_b, mv_b, layer, wo, tm, pre=None):
    m = x.shape[0]
    per_b = SEQ // tm
    row = pl.BlockSpec((tm, D_MODEL), lambda i: (i, 0))
    wspec = pl.BlockSpec((None, D_MODEL, D_MODEL), lambda i: (layer, 0, 0))
    mem = pl.BlockSpec((None, N_MEM, D_MODEL), lambda i: (layer, i // per_b, 0))
    in_specs = [row, pl.BlockSpec((1, D_MODEL), lambda i: (0, 0)), wspec, mem, mem, wspec]
    args = [x, g, wq, mk_b, mv_b, wo]
    if pre is not None:
        a, wa, la = pre
        in_specs += [row, pl.BlockSpec((None, D_MODEL, D_MODEL), lambda i: (la, 0, 0))]
        args += [a, wa]
    return pl.pallas_call(
        _xattn_p_kernel,
        grid=(m // tm,),
        in_specs=in_specs,
        out_specs=row,
        out_shape=jax.ShapeDtypeStruct((m, D_MODEL), F32),
        compiler_params=_params(("parallel",)),
        name="xattn_prompt",
    )(*args)


def _xattn_s_kernel(x_ref, g_ref, wq_ref, mk_ref, mv_ref, wo_ref, o_ref, q_ref, a_ref):
    b = pl.program_id(0)

    @pl.when(b == 0)
    def _():
        h = _rms(x_ref[...], g_ref[...]).astype(BF16)
        q_ref[...] = _dot(h, wq_ref[...]) * (XHEAD_DIM ** -0.5)

    r0 = pl.multiple_of(b * DEC_SEQ, DEC_SEQ)
    q = q_ref[pl.ds(r0, DEC_SEQ), :].astype(BF16)
    def head(ref, hd):
        return jnp.concatenate([ref[pl.ds(c * N_XHEADS + hd, N_MEM, stride=HT_ROWS), :]
                                for c in range(XHEAD_DIM // LANES)], axis=1).astype(BF16)

    outs = []
    for hd in range(N_XHEADS):
        outs.append(_xhead(q[:, hd * XHEAD_DIM:(hd + 1) * XHEAD_DIM], head(mk_ref, hd), head(mv_ref, hd)))
    a_ref[pl.ds(r0, DEC_SEQ), :] = jnp.concatenate(outs, axis=-1)

    @pl.when(b == pl.num_programs(0) - 1)
    def _():
        o_ref[...] = x_ref[...] + _dot(a_ref[...].astype(BF16), wo_ref[...])


def _xattn_sample(x, g, wq, mem_k, mem_v, layer, wo):
    m = x.shape[0]
    full = pl.BlockSpec((m, D_MODEL), lambda b: (0, 0))
    wspec = pl.BlockSpec((None, D_MODEL, D_MODEL), lambda b: (layer, 0, 0))
    mem = pl.BlockSpec((None, None, N_MEM * HT_ROWS, LANES), lambda b: (layer, b, 0, 0))
    return pl.pallas_call(
        _xattn_s_kernel,
        grid=(DEC_BATCH,),
        in_specs=[full, pl.BlockSpec((1, D_MODEL), lambda b: (0, 0)), wspec, mem, mem, wspec],
        out_specs=full,
        out_shape=jax.ShapeDtypeStruct((m, D_MODEL), F32),
        scratch_shapes=[pltpu.VMEM((m, D_MODEL), F32), pltpu.VMEM((m, D_MODEL), F32)],
        compiler_params=_params(("arbitrary",)),
        name="xattn_sample",
    )(x, g, wq, mem_k, mem_v, wo)


def _vec(v):
    return v.reshape(1, -1).astype(F32)


def kernel(x_prompt, x_sample, cache_k_diff, cache_v_diff, state_conv, cache_mem_k, cache_mem_v, page_table, mem_prompt, ffn_norm, ffn_w_in, ffn_w_out, mix_norm, diff_w_qkv, diff_lambda, diff_subln, diff_w_o, rel_bias, conv_w_pw1, conv_b_pw1, conv_w_dw, conv_b_dw, conv_ln_g, conv_ln_b, conv_w_pw2, conv_b_pw2, xattn_norm, xattn_w_q, xattn_w_kv, xattn_w_o, final_norm):
    mp = BATCH * SEQ
    ms = DEC_BATCH * DEC_SEQ
    xp = x_prompt.reshape(mp, D_MODEL)
    xs = x_sample.reshape(ms, D_MODEL)
    tm_p, tm_s = ROW_TILE, ms

    w_in = ffn_w_in.astype(BF16)
    w_out = ffn_w_out.astype(BF16)
    wqkv = diff_w_qkv.astype(BF16)
    wqkv_t = jnp.swapaxes(diff_w_qkv, 1, 2).astype(BF16)
    w_do = diff_w_o.astype(BF16)
    w_pw1 = conv_w_pw1.astype(BF16)
    w_pw2 = conv_w_pw2.astype(BF16)
    w_xq = xattn_w_q.astype(BF16)
    w_xkv = xattn_w_kv.astype(BF16)
    w_xo = xattn_w_o.astype(BF16)

    kk = jnp.arange(ATT_TK)[:, None]
    qq = jnp.arange(ATT_TQ)[None, :]
    d0 = qq - kk
    bucket_p = jnp.concatenate([jnp.where(d0 >= 0, _rel_bucket(jnp.maximum(d0, 0)), -1),
                                _rel_bucket(d0 + ATT_TK)], axis=0).astype(jnp.int32)
    bucket_p = jnp.concatenate([bucket_p, bucket_p], axis=1)
    bias_p = _bias_tiles(rel_bias, bucket_p, LOG2E).reshape(N_HEADS, 2, ATT_TK, 2 * ATT_TQ)
    rs = jnp.arange(DEC_SEQ)[:, None]
    cs = jnp.arange(PAGE_SIZE)[None, :]
    d_new = rs - cs
    bucket_s = jnp.concatenate(
        [_rel_bucket(PAGE_SIZE + rs - cs),
         jnp.where((d_new >= 0) & (cs < DEC_SEQ), _rel_bucket(jnp.maximum(d_new, 0)), -1)],
        axis=1).astype(jnp.int32)
    bias_s = _bias_tiles(rel_bias, bucket_s)
    bias_s = jnp.broadcast_to(bias_s[:, None], (N_HEADS, 2, DEC_SEQ, 2 * PAGE_SIZE)).reshape(
        QROWS, 2 * PAGE_SIZE)

    n_pool = cache_k_diff.shape[1]
    ckt = cache_k_diff.transpose(0, 1, 3, 4, 2).reshape(-1, n_pool, D_MODEL, PAGE_SIZE)
    cv = cache_v_diff.reshape(-1, n_pool, PAGE_SIZE * N_HEADS, DV)
    pt = page_table.reshape(-1).astype(jnp.int32)

    mem2d = mem_prompt.reshape(BATCH * N_MEM, D_MODEL)
    mk, mk_b = _mem_kv(mem2d, w_xkv, 0, 512)
    mv, mv_b = _mem_kv(mem2d, w_xkv, 1, 512)
    cmk = _head_tile_view(cache_mem_k)
    cmv = _head_tile_view(cache_mem_v)

    n_attn = (DEPTH + N_MIXERS - 1) // N_MIXERS
    gf = _vec(final_norm)
    kv_p = None
    conv_p, k_s, v_s, conv_s = [], [], [], []
    for i in range(DEPTH):
        g = _vec(ffn_norm[i, 0])
        xs = _ffn(xs, g, w_in, w_out, i, 0, tm_s)
        gm = _vec(mix_norm[i])
        pre = None
        if i % N_MIXERS == 0:
            a = i // N_MIXERS
            lam_init = 0.8 - 0.6 * math.exp(-0.3 * i)
            lp = diff_lambda[a].astype(F32)
            gs = _vec(diff_subln[a])
            q, k, v = _qkv_sample(xs, gm, wqkv, a)
            k_s.append(k)
            v_s.append(v)
            os_ = _attn_sample(pt, q, ckt, cv, a, k, v, bias_s, lp, gs, lam_init)
            xs = _proj_res(os_, w_do, a, xs, tm_s)
            xp = _ffn(xp, g, w_in, w_out, i, 0, FFN_TM)
            qt, kt_all, kb, v_all, vt = _qkv_prompt(xp, gm, wqkv, wqkv_t, a, n_attn, QKV_TM, prev=kv_p)
            kv_p = (kt_all, v_all)
            op = _attn_prompt(qt, kb, vt, bias_p, lp, gs, lam_init)
            pre = (op, w_do, a)
        else:
            xp = _ffn(xp, g, w_in, w_out, i, 0, FFN_TM)
            cidx = i // N_MIXERS
            cw = (w_pw1, _vec(conv_b_pw1[cidx]),
                  jnp.pad(conv_w_dw[cidx], ((0, HIST_PAD - CONV_W), (0, 0))), _vec(conv_b_dw[cidx]),
                  _vec(conv_ln_g[cidx]), _vec(conv_ln_b[cidx]),
                  w_pw2, _vec(conv_b_pw2[cidx]))
            pad = HIST_PAD - (CONV_W - 1)
            hist_p = jnp.zeros((BATCH, HIST_PAD, D_MODEL), F32)
            hist_s = jnp.pad(state_conv[cidx], ((0, 0), (pad, 0), (0, 0)))
            xp, st_p = _conv_module(xp, hist_p, gm, *cw, cidx, BATCH, 1, SEQ, 512)
            xs, st_s = _conv_module(xs, hist_s, gm, *cw, cidx, DEC_BATCH, DEC_BATCH, DEC_SEQ, DEC_SEQ)
            conv_p.append(st_p[:, pad:])
            conv_s.append(st_s[:, pad:])
        gx = _vec(xattn_norm[i])
        xp = _xattn_prompt(xp, gx, w_xq, mk_b, mv_b, i, w_xo, tm_p, pre=pre)
        xs = _xattn_sample(xs, gx, w_xq, cmk, cmv, i, w_xo)
        g = _vec(ffn_norm[i, 1])
        last = gf if i == DEPTH - 1 else None
        xp = _ffn(xp, g, w_in, w_out, i, 1, FFN_TM, final_g=last)
        xs = _ffn(xs, g, w_in, w_out, i, 1, tm_s, final_g=last)

    y_prompt = xp.reshape(BATCH, SEQ, D_MODEL)
    y_sample = xs.reshape(DEC_BATCH, DEC_SEQ, D_MODEL)
    na = n_attn
    kt_all, v_all = kv_p
    mem_shape = (DEPTH, BATCH, N_MEM, N_XHEADS, XHEAD_DIM)
    new_k_prompt = kt_all.reshape(na, BATCH, 2 * N_HEADS, DQK, SEQ).transpose(0, 1, 4, 2, 3)
    return (y_prompt, y_sample,
            new_k_prompt,
            v_all.reshape(na, BATCH, SEQ, N_HEADS, DV),
            jnp.stack(conv_p),
            _from_head_tile_view(mk).reshape(mem_shape),
            _from_head_tile_view(mv).reshape(mem_shape),
            jnp.stack(k_s).reshape(na, DEC_BATCH, DEC_SEQ, 2 * N_HEADS, DQK),
            jnp.stack(v_s).reshape(na, DEC_BATCH, DEC_SEQ, N_HEADS, DV),
            jnp.stack(conv_s))
```

```python
import functools
import math

import jax
import jax.numpy as jnp
from jax import lax
from jax.experimental import pallas as pl
from jax.experimental.pallas import tpu as pltpu

D_MODEL = 1024
BATCH = 8
SEQ = 2048
DEPTH = 4
DEC_BATCH = 32
DEC_SEQ = 8
PAST_LEN = 8192
PAGE_SIZE = 128
N_MIXERS = 2
N_HEADS = 8
DQK = D_MODEL // N_HEADS // 2
DV = 2 * DQK
N_BUCKETS = 32
MAX_DISTANCE = 128
CONV_W = 31
D_FF = 2816
N_MEM = 256
N_XHEADS = 4
XHEAD_DIM = D_MODEL // N_XHEADS
EPS = 1e-6
LOG2E = math.log2(math.e)

F32 = jnp.float32
BF16 = jnp.bfloat16

VMEM_LIMIT = 56 * 1024 * 1024
LANES = 128
SUBLANES = 8
HIST_PAD = 32
FFN_CHUNKS = ((0, 1536), (1536, D_FF))
ROW_TILE = 1024
FFN_TM = 1024
QKV_TM = 512
ATT_TQ = 256
ATT_TK = 256
PAGES_PER_STEP = 16
HT_ROWS = N_XHEADS * (XHEAD_DIM // LANES)
N_PAGES = PAST_LEN // PAGE_SIZE
QROWS = 2 * N_HEADS * DEC_SEQ
HROWS = 2 * DEC_SEQ
XS_SEQS = 4


def _params(sem):
    return pltpu.CompilerParams(dimension_semantics=sem, vmem_limit_bytes=VMEM_LIMIT)


def _rms(x, g):
    return x * lax.rsqrt(jnp.mean(x * x, axis=-1, keepdims=True) + EPS) * g


def _dot(a, b):
    return jnp.dot(a, b, preferred_element_type=F32)


def _dot_nt(a, b):
    return lax.dot_general(a, b, (((1,), (1,)), ((), ())), preferred_element_type=F32)


def _ffn_kernel(x_ref, g_ref, win_ref, wout_ref, *rest, final):
    x = x_ref[...]
    h = _rms(x, g_ref[...]).astype(BF16)
    acc = None
    for c0, c1 in FFN_CHUNKS:
        a = _dot(h, win_ref[:, c0:c1])
        u = _dot(h, win_ref[:, D_FF + c0:D_FF + c1])
        act = ((a * jax.nn.sigmoid(a)) * u).astype(BF16)
        part = _dot(act, wout_ref[c0:c1, :])
        acc = part if acc is None else acc + part
    y = x + 0.5 * acc
    if final:
        gf_ref, o_ref = rest
        o_ref[...] = _rms(y, gf_ref[...])
    else:
        rest[0][...] = y


def _ffn(x, g, w_in, w_out, layer, half, tm, final_g=None):
    m = x.shape[0]
    row = pl.BlockSpec((tm, D_MODEL), lambda i: (i, 0))
    vec = pl.BlockSpec((1, D_MODEL), lambda i: (0, 0))
    resident = dict(pipeline_mode=pl.Buffered(1))
    in_specs = [row, vec,
                pl.BlockSpec((None, None, D_MODEL, 2 * D_FF), lambda i: (layer, half, 0, 0), **resident),
                pl.BlockSpec((None, None, D_FF, D_MODEL), lambda i: (layer, half, 0, 0), **resident)]
    args = [x, g, w_in, w_out]
    if final_g is not None:
        in_specs.append(vec)
        args.append(final_g)
    return pl.pallas_call(
        functools.partial(_ffn_kernel, final=final_g is not None),
        grid=(m // tm,),
        in_specs=in_specs,
        out_specs=row,
        out_shape=jax.ShapeDtypeStruct((m, D_MODEL), F32),
        compiler_params=_params(("parallel",)),
        name="ffn",
    )(*args)


def _qkv_p_kernel(x_ref, g_ref, wqt_ref, wkt_ref, wv_ref, *rest):
    qt_ref, kt_ref, kb_ref, v_ref, vt_ref = rest[-5:]
    h = _rms(x_ref[...], g_ref[...]).astype(BF16)
    qt_ref[...] = (_dot_nt(wqt_ref[...], h) * (DQK ** -0.5 * LOG2E)).astype(BF16)
    kt = _dot_nt(wkt_ref[...], h)
    kt_ref[...] = kt
    kb_ref[...] = kt.T.astype(BF16)
    v = _dot(h, wv_ref[...])
    vt_ref[...] = v.T.astype(BF16)
    rows = v.shape[0]
    for n in range(N_HEADS):
        v_ref[pl.ds(n, rows, stride=N_HEADS), :] = v[:, n * DV:(n + 1) * DV]


def _qkv_prompt(x, g, wqkv, wqkv_t, a, n_layers, tm, prev=None):
    nt = SEQ // tm
    row = pl.BlockSpec((tm, D_MODEL), lambda b, t: (b * nt + t, 0))
    tr = pl.BlockSpec((None, D_MODEL, tm), lambda b, t: (b, 0, t))

    def wspec(j):
        return pl.BlockSpec((None, D_MODEL, D_MODEL), lambda b, t: (a, 0, j), pipeline_mode=pl.Buffered(1))

    def wtspec(j):
        return pl.BlockSpec((None, D_MODEL, D_MODEL), lambda b, t: (a, j, 0), pipeline_mode=pl.Buffered(1))

    in_specs = [row, pl.BlockSpec((1, D_MODEL), lambda b, t: (0, 0)), wtspec(0), wtspec(1), wspec(2)]
    args = [x, g, wqkv_t, wqkv_t, wqkv]
    aliases = {}
    if prev is not None:
        in_specs += [pl.BlockSpec(memory_space=pl.ANY)] * 2
        aliases = {len(args): 1, len(args) + 1: 3}
        args += list(prev)
    tr_sds = jax.ShapeDtypeStruct((BATCH, D_MODEL, SEQ), BF16)
    return pl.pallas_call(
        _qkv_p_kernel,
        grid=(BATCH, nt),
        in_specs=in_specs,
        out_specs=[tr,
                   pl.BlockSpec((None, None, D_MODEL, tm), lambda b, t: (a, b, 0, t)),
                   row,
                   pl.BlockSpec((None, tm * N_HEADS, DV), lambda b, t: (a, b * nt + t, 0)),
                   tr],
        out_shape=[tr_sds,
                   jax.ShapeDtypeStruct((n_layers, BATCH, D_MODEL, SEQ), F32),
                   jax.ShapeDtypeStruct((BATCH * SEQ, D_MODEL), BF16),
                   jax.ShapeDtypeStruct((n_layers, BATCH * SEQ * N_HEADS, DV), F32),
                   tr_sds],
        input_output_aliases=aliases,
        compiler_params=_params(("parallel", "parallel")),
        name="qkv_prompt",
    )(*args)


def _qkv_s_kernel(x_ref, g_ref, w_ref, q_ref, k_ref, v_ref, h_ref):
    n = pl.program_id(0)

    @pl.when(n == 0)
    def _():
        h_ref[...] = _rms(x_ref[...], g_ref[...]).astype(BF16)

    y = _dot(h_ref[...], w_ref[...])

    @pl.when(n == 0)
    def _():
        q_ref[...] = y * (DQK ** -0.5)

    @pl.when(n == 1)
    def _():
        k_ref[...] = y

    @pl.when(n == 2)
    def _():
        v_ref[...] = y


def _qkv_sample(x, g, wqkv, a):
    m = x.shape[0]
    full = pl.BlockSpec((m, D_MODEL), lambda n: (0, 0))
    out = jax.ShapeDtypeStruct((m, D_MODEL), F32)
    return pl.pallas_call(
        _qkv_s_kernel,
        grid=(3,),
        in_specs=[full, pl.BlockSpec((1, D_MODEL), lambda n: (0, 0)),
                  pl.BlockSpec((None, D_MODEL, D_MODEL), lambda n: (a, 0, n))],
        out_specs=[full, full, full],
        out_shape=[out, out, out],
        scratch_shapes=[pltpu.VMEM((m, D_MODEL), BF16)],
        compiler_params=_params(("arbitrary",)),
        name="qkv_sample",
    )(x, g, wqkv)


def _bias_kernel(rb_ref, bucket_ref, o_ref, *, scale, lane_copies):
    n = pl.program_id(0)
    bucket = bucket_ref[...]
    acc = jnp.zeros(bucket.shape, F32)
    for b in range(N_BUCKETS):
        acc = jnp.where(bucket == b, rb_ref[b, n], acc)
    tile = jnp.where(bucket < 0, -jnp.inf, (acc - rb_ref[N_BUCKETS - 1, n]) * scale)
    o_ref[0] = jnp.concatenate([tile] * lane_copies, axis=1)


def _bias_tiles(rel_bias, bucket, scale=1.0, lane_copies=1):
    r, c = bucket.shape
    return pl.pallas_call(
        functools.partial(_bias_kernel, scale=scale, lane_copies=lane_copies),
        grid=(N_HEADS,),
        in_specs=[pl.BlockSpec(memory_space=pltpu.SMEM),
                  pl.BlockSpec((r, c), lambda n: (0, 0))],
        out_specs=pl.BlockSpec((1, r, c * lane_copies), lambda n: (n, 0, 0)),
        out_shape=jax.ShapeDtypeStruct((N_HEADS, r, c * lane_copies), F32),
        compiler_params=_params(("arbitrary",)),
        name="bias_tiles",
    )(rel_bias, bucket)


def _rel_bucket(n):
    max_exact = N_BUCKETS // 2
    nf = jnp.maximum(n, 1).astype(F32)
    large = max_exact + (jnp.log(nf / max_exact) / math.log(MAX_DISTANCE / max_exact)
                         * (N_BUCKETS - max_exact)).astype(jnp.int32)
    large = jnp.minimum(large, N_BUCKETS - 1)
    return jnp.where(n < max_exact, n, large)


def _lambda(lp_ref, lam_init):
    lp = lp_ref[...]
    s01 = jnp.sum(lp[0:1] * lp[1:2], axis=-1, keepdims=True)
    s23 = jnp.sum(lp[2:3] * lp[3:4], axis=-1, keepdims=True)
    return jnp.exp(s01) - jnp.exp(s23) + lam_init


def _subln(o, g_ref, lam_init):
    return _rms(o, g_ref[...]) * (1.0 - lam_init)


def _fold_rows(s, op):
    acc = s[0:SUBLANES]
    for r in range(1, s.shape[0] // SUBLANES):
        acc = op(acc, s[r * SUBLANES:(r + 1) * SUBLANES])
    return acc


def _attn_p_scores(c, qt_ref, k_ref, bias_ref, s_ref):
    tq, tk = ATT_TQ, ATT_TK
    qt = qt_ref[:, c * tq:(c + 1) * tq]
    row = lax.broadcasted_iota(jnp.int32, qt.shape, 0)
    zero = jnp.zeros_like(qt)
    qq = jnp.concatenate([jnp.where(row < DQK, qt, zero), jnp.where(row >= DQK, qt, zero)], axis=1)
    m8 = None
    for j in range(c + 1):
        s = _dot(k_ref[j * tk:(j + 1) * tk, :], qq)
        if c - j < 2:
            s = s + bias_ref[0, c - j]
        s_ref[j] = s
        t = _fold_rows(s, jnp.maximum)
        m8 = t if m8 is None else jnp.maximum(m8, t)
    return jnp.max(m8, axis=0, keepdims=True)


def _attn_p_softmax(c, m, s_ref, p_ref):
    tk = ATT_TK
    l8 = None
    for j in range(c + 1):
        p = jnp.exp2(s_ref[j] - m)
        t = _fold_rows(p, jnp.add)
        l8 = t if l8 is None else l8 + t
        p_ref[j * tk:(j + 1) * tk, :] = p.astype(BF16)
    return jnp.sum(l8, axis=0, keepdims=True)


def _attn_p_out(c, l, lam, vt_ref, g_ref, o_ref, p_ref, lam_init):
    tq = ATT_TQ
    kend = (c + 1) * ATT_TK
    ot = _dot(vt_ref[:, :kend], p_ref[:kend, :]) / l
    o = (ot[:, :tq] - lam * ot[:, tq:]).T
    o_ref[c * tq:(c + 1) * tq, :] = _subln(o, g_ref, lam_init).astype(o_ref.dtype)


def _attn_p_kernel(qt_ref, k_ref, vt_ref, bias_ref, lp_ref, g_ref, o_ref, s_ref, p_ref, *, lam_init):
    lam = _lambda(lp_ref, lam_init)
    nq = SEQ // ATT_TQ
    m = _attn_p_scores(0, qt_ref, k_ref, bias_ref, s_ref.at[0])
    for c in range(nq):
        m_next = None
        if c + 1 < nq:
            m_next = _attn_p_scores(c + 1, qt_ref, k_ref, bias_ref, s_ref.at[(c + 1) % 2])
        l = _attn_p_softmax(c, m, s_ref.at[c % 2], p_ref.at[c % 2])
        _attn_p_out(c, l, lam, vt_ref, g_ref, o_ref, p_ref.at[c % 2], lam_init)
        m = m_next


def _attn_prompt(qt, kb, vt, bias, lp, g, lam_init):
    nk = SEQ // ATT_TK
    return pl.pallas_call(
        functools.partial(_attn_p_kernel, lam_init=lam_init),
        grid=(BATCH, N_HEADS),
        in_specs=[
            pl.BlockSpec((None, DV, SEQ), lambda b, n: (b, n, 0)),
            pl.BlockSpec((SEQ, DV), lambda b, n: (b, n)),
            pl.BlockSpec((None, DV, SEQ), lambda b, n: (b, n, 0)),
            pl.BlockSpec((1, 2, ATT_TK, 2 * ATT_TQ), lambda b, n: (n, 0, 0, 0)),
            pl.BlockSpec((4, DQK), lambda b, n: (0, 0)),
            pl.BlockSpec((1, DV), lambda b, n: (0, 0)),
        ],
        out_specs=pl.BlockSpec((SEQ, DV), lambda b, n: (b, n)),
        out_shape=jax.ShapeDtypeStruct((BATCH * SEQ, D_MODEL), BF16),
        scratch_shapes=[pltpu.VMEM((2, nk, ATT_TK, 2 * ATT_TQ), F32),
                        pltpu.VMEM((2, SEQ, 2 * ATT_TQ), BF16)],
        compiler_params=_params(("parallel", "parallel")),
        name="attn_prompt",
    )(qt, kb, vt, bias, lp, g)


def _attn_s_init(q_ref, qbd_ref, m_ref, l_ref, acc_ref):
    q = q_ref[...]
    qt = jnp.broadcast_to(q[None], (2 * N_HEADS, DEC_SEQ, D_MODEL)).reshape(QROWS, D_MODEL)
    r = lax.broadcasted_iota(jnp.int32, (QROWS, D_MODEL), 0)
    c = lax.broadcasted_iota(jnp.int32, (QROWS, D_MODEL), 1)
    qbd_ref[...] = jnp.where((c >> 6) == (r >> 3), qt, 0.0).astype(BF16)
    m_ref[...] = jnp.full(m_ref.shape, -jnp.inf, F32)
    l_ref[...] = jnp.zeros_like(l_ref)
    acc_ref[...] = jnp.zeros_like(acc_ref)


def _attn_s_update(s, v_heads, m_ref, l_ref, acc_ref):
    m_old = m_ref[...]
    m_new = jnp.maximum(m_old, jnp.max(s, axis=-1, keepdims=True))
    alpha = jnp.exp(m_old - m_new)
    p = jnp.exp(s - m_new)
    l_ref[...] = alpha * l_ref[...] + jnp.sum(p, axis=-1, keepdims=True)
    pb = p.astype(BF16)
    pv = jnp.concatenate([_dot(pb[n * HROWS:(n + 1) * HROWS], v_heads[n]) for n in range(N_HEADS)], axis=0)
    acc_ref[...] = alpha * acc_ref[...] + pv
    m_ref[...] = m_new


def _attn_s_pages(qbd, k_refs, v_refs):
    ktb = jnp.concatenate([r[0, 0].astype(BF16) for r in k_refs], axis=1)
    s = _dot(qbd, ktb)
    v_heads = [jnp.concatenate([r[0, 0, pl.ds(n, PAGE_SIZE, stride=N_HEADS), :].astype(BF16)
                                for r in v_refs], axis=0) for n in range(N_HEADS)]
    return s, v_heads


def _attn_s_finish(qbd, knew_ref, vnew_ref, bias_ref, lp_ref, g_ref, o_ref, m_ref, l_ref, acc_ref, lam_init):
    pad = jnp.zeros((PAGE_SIZE - DEC_SEQ, D_MODEL), F32)
    kn = jnp.concatenate([knew_ref[...], pad], axis=0).astype(BF16)
    vn = jnp.concatenate([vnew_ref[...], pad], axis=0).astype(BF16)
    _attn_s_update(_dot_nt(qbd, kn) + bias_ref[:, PAGE_SIZE:],
                   [vn[:, n * DV:(n + 1) * DV] for n in range(N_HEADS)], m_ref, l_ref, acc_ref)
    o_all = acc_ref[...] / l_ref[...]
    lam = _lambda(lp_ref, lam_init)
    for n in range(N_HEADS):
        o0 = o_all[n * HROWS:n * HROWS + DEC_SEQ]
        o1 = o_all[n * HROWS + DEC_SEQ:(n + 1) * HROWS]
        o_ref[:, n * DV:(n + 1) * DV] = _subln(o0 - lam * o1, g_ref, lam_init)


def _attn_s_kernel(pt_ref, q_ref, *refs, lam_init):
    del pt_ref
    npg = PAGES_PER_STEP
    k_refs = refs[:npg]
    v_refs = refs[npg:2 * npg]
    (knew_ref, vnew_ref, bias_ref, lp_ref, g_ref, o_ref,
     qbd_ref, m_ref, l_ref, acc_ref) = refs[2 * npg:]
    g = pl.program_id(1)
    ng = pl.num_programs(1)

    pl.when(g == 0)(functools.partial(_attn_s_init, q_ref, qbd_ref, m_ref, l_ref, acc_ref))
    qbd = qbd_ref[...]
    s, v_heads = _attn_s_pages(qbd, k_refs, v_refs)
    last = jnp.where(g == ng - 1, bias_ref[:, :PAGE_SIZE], 0.0)
    s = jnp.concatenate([s[:, :(npg - 1) * PAGE_SIZE], s[:, (npg - 1) * PAGE_SIZE:] + last], axis=1)
    _attn_s_update(s, v_heads, m_ref, l_ref, acc_ref)
    pl.when(g == ng - 1)(functools.partial(_attn_s_finish, qbd, knew_ref, vnew_ref, bias_ref, lp_ref, g_ref,
                                           o_ref, m_ref, l_ref, acc_ref, lam_init))


def _attn_sample(page_table, q, cache_kt, cache_v, layer, knew, vnew, bias, lp, g, lam_init):
    npg = PAGES_PER_STEP
    ng = N_PAGES // npg

    def page_spec(i, rows):
        return pl.BlockSpec((1, 1, rows, LANES),
                            lambda b, gg, pt: (layer, pt[b * N_PAGES + gg * npg + i], 0, 0))

    row8 = pl.BlockSpec((DEC_SEQ, D_MODEL), lambda b, gg, pt: (b, 0))
    grid_spec = pltpu.PrefetchScalarGridSpec(
        num_scalar_prefetch=1,
        grid=(DEC_BATCH, ng),
        in_specs=([row8] + [page_spec(i, D_MODEL) for i in range(npg)]
                  + [page_spec(i, PAGE_SIZE * N_HEADS) for i in range(npg)] + [
            row8, row8,
            pl.BlockSpec((QROWS, 2 * PAGE_SIZE), lambda b, gg, pt: (0, 0)),
            pl.BlockSpec((4, DQK), lambda b, gg, pt: (0, 0)),
            pl.BlockSpec((1, DV), lambda b, gg, pt: (0, 0)),
        ]),
        out_specs=row8,
        scratch_shapes=[pltpu.VMEM((QROWS, D_MODEL), BF16),
                        pltpu.VMEM((QROWS, 1), F32),
                        pltpu.VMEM((QROWS, 1), F32),
                        pltpu.VMEM((QROWS, DV), F32)],
    )
    return pl.pallas_call(
        functools.partial(_attn_s_kernel, lam_init=lam_init),
        grid_spec=grid_spec,
        out_shape=jax.ShapeDtypeStruct((DEC_BATCH * DEC_SEQ, D_MODEL), F32),
        compiler_params=_params(("parallel", "arbitrary")),
        name="attn_sample",
    )(page_table, q, *([cache_kt] * npg), *([cache_v] * npg), knew, vnew, bias, lp, g)


def _proj_res_kernel(a_ref, w_ref, x_ref, o_ref):
    o_ref[...] = x_ref[...] + _dot(a_ref[...].astype(BF16), w_ref[...])


def _proj_res(a, w, layer, x, tm):
    m = x.shape[0]
    row = pl.BlockSpec((tm, D_MODEL), lambda i: (i, 0))
    return pl.pallas_call(
        _proj_res_kernel,
        grid=(m // tm,),
        in_specs=[row, pl.BlockSpec((None, D_MODEL, D_MODEL), lambda i: (layer, 0, 0)), row],
        out_specs=row,
        out_shape=jax.ShapeDtypeStruct((m, D_MODEL), F32),
        compiler_params=_params(("parallel",)),
        name="proj_res",
    )(a, w, x)


def _conv_kernel(x_ref, hist_ref, g_ref, w1_ref, b1_ref, wdw_ref, bdw_ref, lng_ref, lnb_ref,
                 w2_ref, b2_ref, o_ref, st_ref, buf_ref, *win_refs, nbb, tt):
    t = pl.program_id(1)

    @pl.when(t == 0)
    def _():
        buf_ref[:, 0:HIST_PAD, :] = hist_ref[...]

    @pl.when(t > 0)
    def _():
        buf_ref[:, 0:HIST_PAD, :] = buf_ref[:, tt:tt + HIST_PAD, :]

    x = x_ref[...]
    h = _rms(x, g_ref[...]).astype(BF16)
    ag = _dot(h, w1_ref[...]) + b1_ref[...]
    u = ag[:, :D_MODEL] * jax.nn.sigmoid(ag[:, D_MODEL:])
    buf_ref[:, HIST_PAD:, :] = u.reshape(nbb, tt, D_MODEL)

    off = HIST_PAD - (CONV_W - 1)
    y = None
    for b in range(SUBLANES):
        es = [e for e in range(off, off + CONV_W) if e % SUBLANES == b]
        if b == 0:
            win_ref = buf_ref
        else:
            win_ref = win_refs[b % 2]
            win_ref[...] = buf_ref[:, b:b + win_ref.shape[1], :]
        for e in es:
            term = win_ref[:, e - b:e - b + tt, :] * wdw_ref[e - off:e - off + 1, :]
            y = term if y is None else y + term
    y = y.reshape(nbb * tt, D_MODEL) + bdw_ref[...]
    mu = jnp.mean(y, axis=-1, keepdims=True)
    yc = y - mu
    z = yc * lax.rsqrt(jnp.mean(yc * yc, axis=-1, keepdims=True) + EPS) * lng_ref[...] + lnb_ref[...]
    z = z * jax.nn.sigmoid(z)
    o_ref[...] = x + _dot(z.astype(BF16), w2_ref[...]) + b2_ref[...]
    st_ref[...] = buf_ref[:, tt:tt + HIST_PAD, :]


def _conv_module(x, hist, g, w1, b1, wdw, bdw, lng, lnb, w2, b2, cidx, nb, nbb, t_len, tt):
    nt = t_len // tt
    row = pl.BlockSpec((nbb * tt, D_MODEL), lambda b, t: (b * nt + t, 0))
    vec = pl.BlockSpec((1, D_MODEL), lambda b, t: (0, 0))
    st = pl.BlockSpec((nbb, HIST_PAD, D_MODEL), lambda b, t: (b, 0, 0))
    return pl.pallas_call(
        functools.partial(_conv_kernel, nbb=nbb, tt=tt),
        grid=(nb // nbb, nt),
        in_specs=[row, st, vec,
                  pl.BlockSpec((None, D_MODEL, 2 * D_MODEL), lambda b, t: (cidx, 0, 0)),
                  pl.BlockSpec((1, 2 * D_MODEL), lambda b, t: (0, 0)),
                  pl.BlockSpec((HIST_PAD, D_MODEL), lambda b, t: (0, 0)),
                  vec, vec, vec,
                  pl.BlockSpec((None, D_MODEL, D_MODEL), lambda b, t: (cidx, 0, 0)),
                  vec],
        out_specs=[row, st],
        out_shape=[jax.ShapeDtypeStruct((nb * t_len, D_MODEL), F32),
                   jax.ShapeDtypeStruct((nb, HIST_PAD, D_MODEL), F32)],
        scratch_shapes=[pltpu.VMEM((nbb, HIST_PAD + tt, D_MODEL), F32)]
                       + [pltpu.VMEM((nbb, HIST_PAD - SUBLANES + tt, D_MODEL), F32)] * 2,
        compiler_params=_params(("parallel", "arbitrary")),
        name="conv_module",
    )(x, hist, g, w1, b1, wdw, bdw, lng, lnb, w2, b2)


def _head_tile_view(a):
    lead, m = a.shape[:-3], a.shape[-3]
    a = a.reshape(lead + (m, N_XHEADS, XHEAD_DIM // LANES, LANES))
    return jnp.swapaxes(a, -3, -2).reshape(lead + (m * HT_ROWS, LANES))


def _from_head_tile_view(x):
    lead, m = x.shape[:-2], x.shape[-2] // HT_ROWS
    x = x.reshape(lead + (m, XHEAD_DIM // LANES, N_XHEADS, LANES))
    return jnp.swapaxes(x, -3, -2).reshape(lead + (m, N_XHEADS, XHEAD_DIM))


def _memkv_kernel(m_ref, wk_ref, wv_ref, ok_ref, okb_ref, ov_ref, ovb_ref):
    mem = m_ref[...].astype(BF16)
    for w_ref, o_ref, ob_ref in ((wk_ref, ok_ref, okb_ref), (wv_ref, ov_ref, ovb_ref)):
        y = _dot(mem, w_ref[...])
        ob_ref[...] = y.astype(BF16)
        rows = y.shape[0]
        for hd in range(N_XHEADS):
            for c in range(XHEAD_DIM // LANES):
                col = hd * XHEAD_DIM + c * LANES
                o_ref[pl.ds(c * N_XHEADS + hd, rows, stride=HT_ROWS), :] = y[:, col:col + LANES]


def _mem_kv(mem, w_kv, tm):
    m = mem.shape[0]
    view = pl.BlockSpec((None, tm * HT_ROWS, LANES), lambda i, r: (i, r, 0))
    rowm = pl.BlockSpec((None, tm, D_MODEL), lambda i, r: (i, r, 0))
    view_sds = jax.ShapeDtypeStruct((DEPTH, m * HT_ROWS, LANES), F32)
    rowm_sds = jax.ShapeDtypeStruct((DEPTH, m, D_MODEL), BF16)
    return pl.pallas_call(
        _memkv_kernel,
        grid=(DEPTH, m // tm),
        in_specs=[pl.BlockSpec((tm, D_MODEL), lambda i, r: (r, 0)),
                  pl.BlockSpec((None, D_MODEL, D_MODEL), lambda i, r: (i, 0, 0)),
                  pl.BlockSpec((None, D_MODEL, D_MODEL), lambda i, r: (i, 0, 1))],
        out_specs=[view, rowm, view, rowm],
        out_shape=[view_sds, rowm_sds, view_sds, rowm_sds],
        compiler_params=_params(("parallel", "parallel")),
        name="mem_kv",
    )(mem, w_kv, w_kv)


def _xhead(q, mk, mv):
    s = _dot_nt(q, mk)
    p = jnp.exp(s - jnp.max(s, axis=-1, keepdims=True))
    l = jnp.sum(p, axis=-1, keepdims=True)
    return _dot(p.astype(BF16), mv) / l


def _xattn_p_kernel(x_ref, g_ref, wq_ref, mk_ref, mv_ref, wo_ref, *rest):
    o_ref = rest[-1]
    x = x_ref[...]
    if len(rest) == 3:
        a_ref, wa_ref = rest[:2]
        x = x + _dot(a_ref[...], wa_ref[...])
    h = _rms(x, g_ref[...]).astype(BF16)
    q = (_dot(h, wq_ref[...]) * (XHEAD_DIM ** -0.5)).astype(BF16)
    outs = []
    for hd in range(N_XHEADS):
        cs = slice(hd * XHEAD_DIM, (hd + 1) * XHEAD_DIM)
        outs.append(_xhead(q[:, cs], mk_ref[:, cs], mv_ref[:, cs]))
    o = jnp.concatenate(outs, axis=-1)
    o_ref[...] = x + _dot(o.astype(BF16), wo_ref[...])


def _xattn_prompt(x, g, wq, mk_b, mv_b, layer, wo, tm, pre=None):
    m = x.shape[0]
    per_b = SEQ // tm
    row = pl.BlockSpec((tm, D_MODEL), lambda i: (i, 0))
    wspec = pl.BlockSpec((None, D_MODEL, D_MODEL), lambda i: (layer, 0, 0))
    mem = pl.BlockSpec((None, N_MEM, D_MODEL), lambda i: (layer, i // per_b, 0))
    in_specs = [row, pl.BlockSpec((1, D_MODEL), lambda i: (0, 0)), wspec, mem, mem, wspec]
    args = [x, g, wq, mk_b, mv_b, wo]
    if pre is not None:
        a, wa, la = pre
        in_specs += [row, pl.BlockSpec((None, D_MODEL, D_MODEL), lambda i: (la, 0, 0))]
        args += [a, wa]
    return pl.pallas_call(
        _xattn_p_kernel,
        grid=(m // tm,),
        in_specs=in_specs,
        out_specs=row,
        out_shape=jax.ShapeDtypeStruct((m, D_MODEL), F32),
        compiler_params=_params(("parallel",)),
        name="xattn_prompt",
    )(*args)


def _xattn_s_kernel(x_ref, g_ref, wq_ref, mk_ref, mv_ref, wo_ref, o_ref, q_ref, a_ref):
    b = pl.program_id(0)

    @pl.when(b == 0)
    def _():
        h = _rms(x_ref[...], g_ref[...]).astype(BF16)
        q_ref[...] = _dot(h, wq_ref[...]) * (XHEAD_DIM ** -0.5)

    def head(ref, s, hd):
        return jnp.concatenate([ref[s, pl.ds(c * N_XHEADS + hd, N_MEM, stride=HT_ROWS), :]
                                for c in range(XHEAD_DIM // LANES)], axis=1).astype(BF16)

    rows = XS_SEQS * DEC_SEQ
    r0 = pl.multiple_of(b * rows, rows)
    q = q_ref[pl.ds(r0, rows), :]
    seqs = []
    for s in range(XS_SEQS):
        qs = q[s * DEC_SEQ:(s + 1) * DEC_SEQ].astype(BF16)
        seqs.append(jnp.concatenate(
            [_xhead(qs[:, hd * XHEAD_DIM:(hd + 1) * XHEAD_DIM], head(mk_ref, s, hd), head(mv_ref, s, hd))
             for hd in range(N_XHEADS)], axis=-1))
    a_ref[pl.ds(r0, rows), :] = jnp.concatenate(seqs, axis=0)

    @pl.when(b == pl.num_programs(0) - 1)
    def _():
        o_ref[...] = x_ref[...] + _dot(a_ref[...].astype(BF16), wo_ref[...])


def _xattn_sample(x, g, wq, mem_k, mem_v, layer, wo):
    m = x.shape[0]
    full = pl.BlockSpec((m, D_MODEL), lambda b: (0, 0))
    wspec = pl.BlockSpec((None, D_MODEL, D_MODEL), lambda b: (layer, 0, 0))
    mem = pl.BlockSpec((None, XS_SEQS, N_MEM * HT_ROWS, LANES), lambda b: (layer, b, 0, 0))
    return pl.pallas_call(
        _xattn_s_kernel,
        grid=(DEC_BATCH // XS_SEQS,),
        in_specs=[full, pl.BlockSpec((1, D_MODEL), lambda b: (0, 0)), wspec, mem, mem, wspec],
        out_specs=full,
        out_shape=jax.ShapeDtypeStruct((m, D_MODEL), F32),
        scratch_shapes=[pltpu.VMEM((m, D_MODEL), F32), pltpu.VMEM((m, D_MODEL), F32)],
        compiler_params=_params(("arbitrary",)),
        name="xattn_sample",
    )(x, g, wq, mem_k, mem_v, wo)


def _vec(v):
    return v.reshape(1, -1).astype(F32)


def kernel(x_prompt, x_sample, cache_k_diff, cache_v_diff, state_conv, cache_mem_k, cache_mem_v, page_table, mem_prompt, ffn_norm, ffn_w_in, ffn_w_out, mix_norm, diff_w_qkv, diff_lambda, diff_subln, diff_w_o, rel_bias, conv_w_pw1, conv_b_pw1, conv_w_dw, conv_b_dw, conv_ln_g, conv_ln_b, conv_w_pw2, conv_b_pw2, xattn_norm, xattn_w_q, xattn_w_kv, xattn_w_o, final_norm):
    mp = BATCH * SEQ
    ms = DEC_BATCH * DEC_SEQ
    xp = x_prompt.reshape(mp, D_MODEL)
    xs = x_sample.reshape(ms, D_MODEL)
    tm_p, tm_s = ROW_TILE, ms

    w_in = ffn_w_in.astype(BF16)
    w_out = ffn_w_out.astype(BF16)
    wqkv = diff_w_qkv.astype(BF16)
    wqkv_t = jnp.swapaxes(diff_w_qkv, 1, 2).astype(BF16)
    w_do = diff_w_o.astype(BF16)
    w_pw1 = conv_w_pw1.astype(BF16)
    w_pw2 = conv_w_pw2.astype(BF16)
    w_xq = xattn_w_q.astype(BF16)
    w_xkv = xattn_w_kv.astype(BF16)
    w_xo = xattn_w_o.astype(BF16)

    kk = jnp.arange(ATT_TK)[:, None]
    qq = jnp.arange(ATT_TQ)[None, :]
    d0 = qq - kk
    bucket_p = jnp.concatenate([jnp.where(d0 >= 0, _rel_bucket(jnp.maximum(d0, 0)), -1),
                                _rel_bucket(d0 + ATT_TK)], axis=0).astype(jnp.int32)
    bias_p = _bias_tiles(rel_bias, bucket_p, LOG2E, 2).reshape(N_HEADS, 2, ATT_TK, 2 * ATT_TQ)
    rs = jnp.arange(DEC_SEQ)[:, None]
    cs = jnp.arange(PAGE_SIZE)[None, :]
    d_new = rs - cs
    bucket_s = jnp.concatenate(
        [_rel_bucket(PAGE_SIZE + rs - cs),
         jnp.where((d_new >= 0) & (cs < DEC_SEQ), _rel_bucket(jnp.maximum(d_new, 0)), -1)],
        axis=1).astype(jnp.int32)
    bias_s = _bias_tiles(rel_bias, bucket_s)
    bias_s = jnp.broadcast_to(bias_s[:, None], (N_HEADS, 2, DEC_SEQ, 2 * PAGE_SIZE)).reshape(
        QROWS, 2 * PAGE_SIZE)

    n_pool = cache_k_diff.shape[1]
    ckt = cache_k_diff.transpose(0, 1, 3, 4, 2).reshape(-1, n_pool, D_MODEL, PAGE_SIZE)
    cv = cache_v_diff.reshape(-1, n_pool, PAGE_SIZE * N_HEADS, DV)
    pt = page_table.reshape(-1).astype(jnp.int32)

    mem2d = mem_prompt.reshape(BATCH * N_MEM, D_MODEL)
    mk, mk_b, mv, mv_b = _mem_kv(mem2d, w_xkv, 512)
    cmk = _head_tile_view(cache_mem_k)
    cmv = _head_tile_view(cache_mem_v)

    n_attn = (DEPTH + N_MIXERS - 1) // N_MIXERS
    gf = _vec(final_norm)
    kv_p = None
    conv_p, k_s, v_s, conv_s = [], [], [], []
    for i in range(DEPTH):
        g = _vec(ffn_norm[i, 0])
        xs = _ffn(xs, g, w_in, w_out, i, 0, tm_s)
        gm = _vec(mix_norm[i])
        pre = None
        if i % N_MIXERS == 0:
            a = i // N_MIXERS
            lam_init = 0.8 - 0.6 * math.exp(-0.3 * i)
            lp = diff_lambda[a].astype(F32)
            gs = _vec(diff_subln[a])
            q, k, v = _qkv_sample(xs, gm, wqkv, a)
            k_s.append(k)
            v_s.append(v)
            os_ = _attn_sample(pt, q, ckt, cv, a, k, v, bias_s, lp, gs, lam_init)
            xs = _proj_res(os_, w_do, a, xs, tm_s)
            xp = _ffn(xp, g, w_in, w_out, i, 0, FFN_TM)
            qt, kt_all, kb, v_all, vt = _qkv_prompt(xp, gm, wqkv, wqkv_t, a, n_attn, QKV_TM, prev=kv_p)
            kv_p = (kt_all, v_all)
            op = _attn_prompt(qt, kb, vt, bias_p, lp, gs, lam_init)
            pre = (op, w_do, a)
        else:
            xp = _ffn(xp, g, w_in, w_out, i, 0, FFN_TM)
            cidx = i // N_MIXERS
            cw = (w_pw1, _vec(conv_b_pw1[cidx]),
                  jnp.pad(conv_w_dw[cidx], ((0, HIST_PAD - CONV_W), (0, 0))), _vec(conv_b_dw[cidx]),
                  _vec(conv_ln_g[cidx]), _vec(conv_ln_b[cidx]),
                  w_pw2, _vec(conv_b_pw2[cidx]))
            pad = HIST_PAD - (CONV_W - 1)
            hist_p = jnp.zeros((BATCH, HIST_PAD, D_MODEL), F32)
            hist_s = jnp.pad(state_conv[cidx], ((0, 0), (pad, 0), (0, 0)))
            xp, st_p = _conv_module(xp, hist_p, gm, *cw, cidx, BATCH, 1, SEQ, 512)
            xs, st_s = _conv_module(xs, hist_s, gm, *cw, cidx, DEC_BATCH, DEC_BATCH, DEC_SEQ, DEC_SEQ)
            conv_p.append(st_p[:, pad:])
            conv_s.append(st_s[:, pad:])
        gx = _vec(xattn_norm[i])
        xp = _xattn_prompt(xp, gx, w_xq, mk_b, mv_b, i, w_xo, tm_p, pre=pre)
        xs = _xattn_sample(xs, gx, w_xq, cmk, cmv, i, w_xo)
        g = _vec(ffn_norm[i, 1])
        last = gf if i == DEPTH - 1 else None
        xp = _ffn(xp, g, w_in, w_out, i, 1, FFN_TM, final_g=last)
        xs = _ffn(xs, g, w_in, w_out, i, 1, tm_s, final_g=last)

    y_prompt = xp.reshape(BATCH, SEQ, D_MODEL)
    y_sample = xs.reshape(DEC_BATCH, DEC_SEQ, D_MODEL)
    na = n_attn
    kt_all, v_all = kv_p
    mem_shape = (DEPTH, BATCH, N_MEM, N_XHEADS, XHEAD_DIM)
    new_k_prompt = kt_all.reshape(na, BATCH, 2 * N_HEADS, DQK, SEQ).transpose(0, 1, 4, 2, 3)
    return (y_prompt, y_sample,
            new_k_prompt,
            v_all.reshape(na, BATCH, SEQ, N_HEADS, DV),
            jnp.stack(conv_p),
            _from_head_tile_view(mk).reshape(mem_shape),
            _from_head_tile_view(mv).reshape(mem_shape),
            jnp.stack(k_s).reshape(na, DEC_BATCH, DEC_SEQ, 2 * N_HEADS, DQK),
            jnp.stack(v_s).reshape(na, DEC_BATCH, DEC_SEQ, N_HEADS, DV),
            jnp.stack(conv_s))
```

```python
import functools
import math

import jax
import jax.numpy as jnp
from jax import lax
from jax.experimental import pallas as pl
from jax.experimental.pallas import tpu as pltpu

D_MODEL = 1024
BATCH = 8
SEQ = 2048
DEPTH = 4
DEC_BATCH = 32
DEC_SEQ = 8
PAST_LEN = 8192
PAGE_SIZE = 128
N_MIXERS = 2
N_HEADS = 8
DQK = D_MODEL // N_HEADS // 2
DV = 2 * DQK
N_BUCKETS = 32
MAX_DISTANCE = 128
CONV_W = 31
D_FF = 2816
N_MEM = 256
N_XHEADS = 4
XHEAD_DIM = D_MODEL // N_XHEADS
EPS = 1e-6
LOG2E = math.log2(math.e)

F32 = jnp.float32
BF16 = jnp.bfloat16

VMEM_LIMIT = 56 * 1024 * 1024
LANES = 128
SUBLANES = 8
HIST_PAD = 32
FFN_CHUNKS = ((0, 1536), (1536, D_FF))
ROW_TILE = 1024
FFN_TM = 1024
QKV_TM = 512
ATT_TQ = 256
ATT_TK = 256
PAGES_PER_STEP = 16
HT_ROWS = N_XHEADS * (XHEAD_DIM // LANES)
N_PAGES = PAST_LEN // PAGE_SIZE
QROWS = 2 * N_HEADS * DEC_SEQ
HROWS = 2 * DEC_SEQ
XS_SEQS = 4


def _params(sem):
    return pltpu.CompilerParams(dimension_semantics=sem, vmem_limit_bytes=VMEM_LIMIT)


def _rms(x, g):
    return x * lax.rsqrt(jnp.mean(x * x, axis=-1, keepdims=True) + EPS) * g


def _dot(a, b):
    return jnp.dot(a, b, preferred_element_type=F32)


def _dot_nt(a, b):
    return lax.dot_general(a, b, (((1,), (1,)), ((), ())), preferred_element_type=F32)


def _ffn_rows(x, g_ref, win_ref, wout_ref, gf_ref):
    h = _rms(x, g_ref[...]).astype(BF16)
    acc = None
    for c0, c1 in FFN_CHUNKS:
        a = _dot(h, win_ref[:, c0:c1])
        u = _dot(h, win_ref[:, D_FF + c0:D_FF + c1])
        act = ((a * jax.nn.sigmoid(a)) * u).astype(BF16)
        part = _dot(act, wout_ref[c0:c1, :])
        acc = part if acc is None else acc + part
    y = x + 0.5 * acc
    return y if gf_ref is None else _rms(y, gf_ref[...])


def _ffn_kernel(x_ref, xs_ref, g_ref, win_ref, wout_ref, *rest, final):
    gf_ref = rest[0] if final else None
    o_ref, os_ref = rest[-2:]
    o_ref[...] = _ffn_rows(x_ref[...], g_ref, win_ref, wout_ref, gf_ref)

    @pl.when(pl.program_id(0) == pl.num_programs(0) - 1)
    def _():
        os_ref[...] = _ffn_rows(xs_ref[...], g_ref, win_ref, wout_ref, gf_ref)


def _ffn(x, xs, g, w_in, w_out, layer, half, tm, final_g=None):
    m, ms = x.shape[0], xs.shape[0]
    row = pl.BlockSpec((tm, D_MODEL), lambda i: (i, 0))
    small = pl.BlockSpec((ms, D_MODEL), lambda i: (0, 0))
    vec = pl.BlockSpec((1, D_MODEL), lambda i: (0, 0))
    resident = dict(pipeline_mode=pl.Buffered(1))
    in_specs = [row, small, vec,
                pl.BlockSpec((None, None, D_MODEL, 2 * D_FF), lambda i: (layer, half, 0, 0), **resident),
                pl.BlockSpec((None, None, D_FF, D_MODEL), lambda i: (layer, half, 0, 0), **resident)]
    args = [x, xs, g, w_in, w_out]
    if final_g is not None:
        in_specs.append(vec)
        args.append(final_g)
    return pl.pallas_call(
        functools.partial(_ffn_kernel, final=final_g is not None),
        grid=(m // tm,),
        in_specs=in_specs,
        out_specs=[row, small],
        out_shape=[jax.ShapeDtypeStruct((m, D_MODEL), F32), jax.ShapeDtypeStruct((ms, D_MODEL), F32)],
        compiler_params=_params(("arbitrary",)),
        name="ffn",
    )(*args)


def _qkv_p_kernel(x_ref, g_ref, wqt_ref, wkt_ref, wv_ref, *rest):
    qt_ref, kt_ref, kb_ref, v_ref, vt_ref = rest[-5:]
    h = _rms(x_ref[...], g_ref[...]).astype(BF16)
    qt_ref[...] = (_dot_nt(wqt_ref[...], h) * (DQK ** -0.5 * LOG2E)).astype(BF16)
    kt = _dot_nt(wkt_ref[...], h)
    kt_ref[...] = kt
    kb_ref[...] = kt.T.astype(BF16)
    v = _dot(h, wv_ref[...])
    vt_ref[...] = v.T.astype(BF16)
    rows = v.shape[0]
    for n in range(N_HEADS):
        v_ref[pl.ds(n, rows, stride=N_HEADS), :] = v[:, n * DV:(n + 1) * DV]


def _qkv_prompt(x, g, wqkv, wqkv_t, a, n_layers, tm, prev=None):
    nt = SEQ // tm
    row = pl.BlockSpec((tm, D_MODEL), lambda b, t: (b * nt + t, 0))
    tr = pl.BlockSpec((None, D_MODEL, tm), lambda b, t: (b, 0, t))

    def wspec(j):
        return pl.BlockSpec((None, D_MODEL, D_MODEL), lambda b, t: (a, 0, j), pipeline_mode=pl.Buffered(1))

    def wtspec(j):
        return pl.BlockSpec((None, D_MODEL, D_MODEL), lambda b, t: (a, j, 0), pipeline_mode=pl.Buffered(1))

    in_specs = [row, pl.BlockSpec((1, D_MODEL), lambda b, t: (0, 0)), wtspec(0), wtspec(1), wspec(2)]
    args = [x, g, wqkv_t, wqkv_t, wqkv]
    aliases = {}
    if prev is not None:
        in_specs += [pl.BlockSpec(memory_space=pl.ANY)] * 2
        aliases = {len(args): 1, len(args) + 1: 3}
        args += list(prev)
    tr_sds = jax.ShapeDtypeStruct((BATCH, D_MODEL, SEQ), BF16)
    return pl.pallas_call(
        _qkv_p_kernel,
        grid=(BATCH, nt),
        in_specs=in_specs,
        out_specs=[tr,
                   pl.BlockSpec((None, None, D_MODEL, tm), lambda b, t: (a, b, 0, t)),
                   row,
                   pl.BlockSpec((None, tm * N_HEADS, DV), lambda b, t: (a, b * nt + t, 0)),
                   tr],
        out_shape=[tr_sds,
                   jax.ShapeDtypeStruct((n_layers, BATCH, D_MODEL, SEQ), F32),
                   jax.ShapeDtypeStruct((BATCH * SEQ, D_MODEL), BF16),
                   jax.ShapeDtypeStruct((n_layers, BATCH * SEQ * N_HEADS, DV), F32),
                   tr_sds],
        input_output_aliases=aliases,
        compiler_params=_params(("parallel", "parallel")),
        name="qkv_prompt",
    )(*args)


def _qkv_s_kernel(x_ref, g_ref, w_ref, q_ref, k_ref, v_ref, h_ref):
    n = pl.program_id(0)

    @pl.when(n == 0)
    def _():
        h_ref[...] = _rms(x_ref[...], g_ref[...]).astype(BF16)

    y = _dot(h_ref[...], w_ref[...])

    @pl.when(n == 0)
    def _():
        q_ref[...] = y * (DQK ** -0.5)

    @pl.when(n == 1)
    def _():
        k_ref[...] = y

    @pl.when(n == 2)
    def _():
        v_ref[...] = y


def _qkv_sample(x, g, wqkv, a):
    m = x.shape[0]
    full = pl.BlockSpec((m, D_MODEL), lambda n: (0, 0))
    out = jax.ShapeDtypeStruct((m, D_MODEL), F32)
    return pl.pallas_call(
        _qkv_s_kernel,
        grid=(3,),
        in_specs=[full, pl.BlockSpec((1, D_MODEL), lambda n: (0, 0)),
                  pl.BlockSpec((None, D_MODEL, D_MODEL), lambda n: (a, 0, n))],
        out_specs=[full, full, full],
        out_shape=[out, out, out],
        scratch_shapes=[pltpu.VMEM((m, D_MODEL), BF16)],
        compiler_params=_params(("arbitrary",)),
        name="qkv_sample",
    )(x, g, wqkv)


def _bias_kernel(rb_ref, bucket_ref, o_ref, *, scale, lane_copies):
    n = pl.program_id(0)
    bucket = bucket_ref[...]
    acc = jnp.zeros(bucket.shape, F32)
    for b in range(N_BUCKETS):
        acc = jnp.where(bucket == b, rb_ref[b, n], acc)
    tile = jnp.where(bucket < 0, -jnp.inf, (acc - rb_ref[N_BUCKETS - 1, n]) * scale)
    o_ref[0] = jnp.concatenate([tile] * lane_copies, axis=1)


def _bias_tiles(rel_bias, bucket, scale=1.0, lane_copies=1):
    r, c = bucket.shape
    return pl.pallas_call(
        functools.partial(_bias_kernel, scale=scale, lane_copies=lane_copies),
        grid=(N_HEADS,),
        in_specs=[pl.BlockSpec(memory_space=pltpu.SMEM),
                  pl.BlockSpec((r, c), lambda n: (0, 0))],
        out_specs=pl.BlockSpec((1, r, c * lane_copies), lambda n: (n, 0, 0)),
        out_shape=jax.ShapeDtypeStruct((N_HEADS, r, c * lane_copies), F32),
        compiler_params=_params(("arbitrary",)),
        name="bias_tiles",
    )(rel_bias, bucket)


def _rel_bucket(n):
    max_exact = N_BUCKETS // 2
    nf = jnp.maximum(n, 1).astype(F32)
    large = max_exact + (jnp.log(nf / max_exact) / math.log(MAX_DISTANCE / max_exact)
                         * (N_BUCKETS - max_exact)).astype(jnp.int32)
    large = jnp.minimum(large, N_BUCKETS - 1)
    return jnp.where(n < max_exact, n, large)


def _lambda(lp_ref, lam_init):
    lp = lp_ref[...]
    s01 = jnp.sum(lp[0:1] * lp[1:2], axis=-1, keepdims=True)
    s23 = jnp.sum(lp[2:3] * lp[3:4], axis=-1, keepdims=True)
    return jnp.exp(s01) - jnp.exp(s23) + lam_init


def _subln(o, g_ref, lam_init):
    return _rms(o, g_ref[...]) * (1.0 - lam_init)


def _fold_rows(s, op):
    acc = s[0:SUBLANES]
    for r in range(1, s.shape[0] // SUBLANES):
        acc = op(acc, s[r * SUBLANES:(r + 1) * SUBLANES])
    return acc


def _attn_p_scores(c, qt_ref, k_ref, bias_ref, s_ref):
    tq, tk = ATT_TQ, ATT_TK
    qt = qt_ref[:, c * tq:(c + 1) * tq]
    row = lax.broadcasted_iota(jnp.int32, qt.shape, 0)
    zero = jnp.zeros_like(qt)
    qq = jnp.concatenate([jnp.where(row < DQK, qt, zero), jnp.where(row >= DQK, qt, zero)], axis=1)
    m8 = None
    for j in range(c + 1):
        s = _dot(k_ref[j * tk:(j + 1) * tk, :], qq)
        if c - j < 2:
            s = s + bias_ref[0, c - j]
        s_ref[j] = s
        t = _fold_rows(s, jnp.maximum)
        m8 = t if m8 is None else jnp.maximum(m8, t)
    return jnp.max(m8, axis=0, keepdims=True)


def _attn_p_softmax(c, m, s_ref, p_ref):
    tk = ATT_TK
    l8 = None
    for j in range(c + 1):
        p = jnp.exp2(s_ref[j] - m)
        t = _fold_rows(p, jnp.add)
        l8 = t if l8 is None else l8 + t
        p_ref[j * tk:(j + 1) * tk, :] = p.astype(BF16)
    return jnp.sum(l8, axis=0, keepdims=True)


def _attn_p_out(c, l, lam, vt_ref, g_ref, o_ref, p_ref, lam_init):
    tq = ATT_TQ
    kend = (c + 1) * ATT_TK
    ot = _dot(vt_ref[:, :kend], p_ref[:kend, :]) / l
    o = (ot[:, :tq] - lam * ot[:, tq:]).T
    o_ref[c * tq:(c + 1) * tq, :] = _subln(o, g_ref, lam_init).astype(o_ref.dtype)


def _attn_p_kernel(qt_ref, k_ref, vt_ref, bias_ref, lp_ref, g_ref, o_ref, s_ref, p_ref, *, lam_init):
    lam = _lambda(lp_ref, lam_init)
    nq = SEQ // ATT_TQ
    m = _attn_p_scores(0, qt_ref, k_ref, bias_ref, s_ref.at[0])
    for c in range(nq):
        m_next = None
        if c + 1 < nq:
            m_next = _attn_p_scores(c + 1, qt_ref, k_ref, bias_ref, s_ref.at[(c + 1) % 2])
        l = _attn_p_softmax(c, m, s_ref.at[c % 2], p_ref.at[c % 2])
        _attn_p_out(c, l, lam, vt_ref, g_ref, o_ref, p_ref.at[c % 2], lam_init)
        m = m_next


def _attn_prompt(qt, kb, vt, bias, lp, g, lam_init):
    nk = SEQ // ATT_TK
    return pl.pallas_call(
        functools.partial(_attn_p_kernel, lam_init=lam_init),
        grid=(BATCH, N_HEADS),
        in_specs=[
            pl.BlockSpec((None, DV, SEQ), lambda b, n: (b, n, 0)),
            pl.BlockSpec((SEQ, DV), lambda b, n: (b, n)),
            pl.BlockSpec((None, DV, SEQ), lambda b, n: (b, n, 0)),
            pl.BlockSpec((1, 2, ATT_TK, 2 * ATT_TQ), lambda b, n: (n, 0, 0, 0)),
            pl.BlockSpec((4, DQK), lambda b, n: (0, 0)),
            pl.BlockSpec((1, DV), lambda b, n: (0, 0)),
        ],
        out_specs=pl.BlockSpec((SEQ, DV), lambda b, n: (b, n)),
        out_shape=jax.ShapeDtypeStruct((BATCH * SEQ, D_MODEL), BF16),
        scratch_shapes=[pltpu.VMEM((2, nk, ATT_TK, 2 * ATT_TQ), F32),
                        pltpu.VMEM((2, SEQ, 2 * ATT_TQ), BF16)],
        compiler_params=_params(("parallel", "parallel")),
        name="attn_prompt",
    )(qt, kb, vt, bias, lp, g)


def _attn_s_init(q_ref, qbd_ref, m_ref, l_ref, acc_ref):
    q = q_ref[...]
    qt = jnp.broadcast_to(q[None], (2 * N_HEADS, DEC_SEQ, D_MODEL)).reshape(QROWS, D_MODEL)
    r = lax.broadcasted_iota(jnp.int32, (QROWS, D_MODEL), 0)
    c = lax.broadcasted_iota(jnp.int32, (QROWS, D_MODEL), 1)
    qbd_ref[...] = jnp.where((c >> 6) == (r >> 3), qt, 0.0).astype(BF16)
    m_ref[...] = jnp.full(m_ref.shape, -jnp.inf, F32)
    l_ref[...] = jnp.zeros_like(l_ref)
    acc_ref[...] = jnp.zeros_like(acc_ref)


def _attn_s_update(s, v_heads, m_ref, l_ref, acc_ref):
    m_old = m_ref[...]
    m_new = jnp.maximum(m_old, jnp.max(s, axis=-1, keepdims=True))
    alpha = jnp.exp(m_old - m_new)
    p = jnp.exp(s - m_new)
    l_ref[...] = alpha * l_ref[...] + jnp.sum(p, axis=-1, keepdims=True)
    pb = p.astype(BF16)
    pv = jnp.concatenate([_dot(pb[n * HROWS:(n + 1) * HROWS], v_heads[n]) for n in range(N_HEADS)], axis=0)
    acc_ref[...] = alpha * acc_ref[...] + pv
    m_ref[...] = m_new


def _attn_s_pages(qbd, k_refs, v_refs):
    ktb = jnp.concatenate([r[0, 0].astype(BF16) for r in k_refs], axis=1)
    s = _dot(qbd, ktb)
    v_heads = [jnp.concatenate([r[0, 0, pl.ds(n, PAGE_SIZE, stride=N_HEADS), :].astype(BF16)
                                for r in v_refs], axis=0) for n in range(N_HEADS)]
    return s, v_heads


def _attn_s_finish(qbd, knew_ref, vnew_ref, bias_ref, lp_ref, g_ref, o_ref, m_ref, l_ref, acc_ref, lam_init):
    pad = jnp.zeros((PAGE_SIZE - DEC_SEQ, D_MODEL), F32)
    kn = jnp.concatenate([knew_ref[...], pad], axis=0).astype(BF16)
    vn = jnp.concatenate([vnew_ref[...], pad], axis=0).astype(BF16)
    _attn_s_update(_dot_nt(qbd, kn) + bias_ref[:, PAGE_SIZE:],
                   [vn[:, n * DV:(n + 1) * DV] for n in range(N_HEADS)], m_ref, l_ref, acc_ref)
    o_all = acc_ref[...] / l_ref[...]
    lam = _lambda(lp_ref, lam_init)
    for n in range(N_HEADS):
        o0 = o_all[n * HROWS:n * HROWS + DEC_SEQ]
        o1 = o_all[n * HROWS + DEC_SEQ:(n + 1) * HROWS]
        o_ref[:, n * DV:(n + 1) * DV] = _subln(o0 - lam * o1, g_ref, lam_init)


def _attn_s_kernel(pt_ref, q_ref, *refs, lam_init):
    del pt_ref
    npg = PAGES_PER_STEP
    k_refs = refs[:npg]
    v_refs = refs[npg:2 * npg]
    (knew_ref, vnew_ref, bias_ref, lp_ref, g_ref, o_ref,
     qbd_ref, m_ref, l_ref, acc_ref) = refs[2 * npg:]
    g = pl.program_id(1)
    ng = pl.num_programs(1)

    pl.when(g == 0)(functools.partial(_attn_s_init, q_ref, qbd_ref, m_ref, l_ref, acc_ref))
    qbd = qbd_ref[...]
    s, v_heads = _attn_s_pages(qbd, k_refs, v_refs)
    last = jnp.where(g == ng - 1, bias_ref[:, :PAGE_SIZE], 0.0)
    s = jnp.concatenate([s[:, :(npg - 1) * PAGE_SIZE], s[:, (npg - 1) * PAGE_SIZE:] + last], axis=1)
    _attn_s_update(s, v_heads, m_ref, l_ref, acc_ref)
    pl.when(g == ng - 1)(functools.partial(_attn_s_finish, qbd, knew_ref, vnew_ref, bias_ref, lp_ref, g_ref,
                                           o_ref, m_ref, l_ref, acc_ref, lam_init))


def _attn_sample(page_table, q, cache_kt, cache_v, layer, knew, vnew, bias, lp, g, lam_init):
    npg = PAGES_PER_STEP
    ng = N_PAGES // npg

    def page_spec(i, rows):
        return pl.BlockSpec((1, 1, rows, LANES),
                            lambda b, gg, pt: (layer, pt[b * N_PAGES + gg * npg + i], 0, 0))

    row8 = pl.BlockSpec((DEC_SEQ, D_MODEL), lambda b, gg, pt: (b, 0))
    grid_spec = pltpu.PrefetchScalarGridSpec(
        num_scalar_prefetch=1,
        grid=(DEC_BATCH, ng),
        in_specs=([row8] + [page_spec(i, D_MODEL) for i in range(npg)]
                  + [page_spec(i, PAGE_SIZE * N_HEADS) for i in range(npg)] + [
            row8, row8,
            pl.BlockSpec((QROWS, 2 * PAGE_SIZE), lambda b, gg, pt: (0, 0)),
            pl.BlockSpec((4, DQK), lambda b, gg, pt: (0, 0)),
            pl.BlockSpec((1, DV), lambda b, gg, pt: (0, 0)),
        ]),
        out_specs=row8,
        scratch_shapes=[pltpu.VMEM((QROWS, D_MODEL), BF16),
                        pltpu.VMEM((QROWS, 1), F32),
                        pltpu.VMEM((QROWS, 1), F32),
                        pltpu.VMEM((QROWS, DV), F32)],
    )
    return pl.pallas_call(
        functools.partial(_attn_s_kernel, lam_init=lam_init),
        grid_spec=grid_spec,
        out_shape=jax.ShapeDtypeStruct((DEC_BATCH * DEC_SEQ, D_MODEL), F32),
        compiler_params=_params(("parallel", "arbitrary")),
        name="attn_sample",
    )(page_table, q, *([cache_kt] * npg), *([cache_v] * npg), knew, vnew, bias, lp, g)


def _proj_res_kernel(a_ref, w_ref, x_ref, o_ref):
    o_ref[...] = x_ref[...] + _dot(a_ref[...].astype(BF16), w_ref[...])


def _proj_res(a, w, layer, x, tm):
    m = x.shape[0]
    row = pl.BlockSpec((tm, D_MODEL), lambda i: (i, 0))
    return pl.pallas_call(
        _proj_res_kernel,
        grid=(m // tm,),
        in_specs=[row, pl.BlockSpec((None, D_MODEL, D_MODEL), lambda i: (layer, 0, 0)), row],
        out_specs=row,
        out_shape=jax.ShapeDtypeStruct((m, D_MODEL), F32),
        compiler_params=_params(("parallel",)),
        name="proj_res",
    )(a, w, x)


def _conv_kernel(x_ref, hist_ref, g_ref, w1_ref, b1_ref, wdw_ref, bdw_ref, lng_ref, lnb_ref,
                 w2_ref, b2_ref, o_ref, st_ref, buf_ref, *win_refs, nbb, tt):
    t = pl.program_id(1)

    @pl.when(t == 0)
    def _():
        buf_ref[:, 0:HIST_PAD, :] = hist_ref[...]

    @pl.when(t > 0)
    def _():
        buf_ref[:, 0:HIST_PAD, :] = buf_ref[:, tt:tt + HIST_PAD, :]

    x = x_ref[...]
    h = _rms(x, g_ref[...]).astype(BF16)
    ag = _dot(h, w1_ref[...]) + b1_ref[...]
    u = ag[:, :D_MODEL] * jax.nn.sigmoid(ag[:, D_MODEL:])
    buf_ref[:, HIST_PAD:, :] = u.reshape(nbb, tt, D_MODEL)

    off = HIST_PAD - (CONV_W - 1)
    y = None
    for b in range(SUBLANES):
        es = [e for e in range(off, off + CONV_W) if e % SUBLANES == b]
        if b == 0:
            win_ref = buf_ref
        else:
            win_ref = win_refs[b % 2]
            win_ref[...] = buf_ref[:, b:b + win_ref.shape[1], :]
        for e in es:
            term = win_ref[:, e - b:e - b + tt, :] * wdw_ref[e - off:e - off + 1, :]
            y = term if y is None else y + term
    y = y.reshape(nbb * tt, D_MODEL) + bdw_ref[...]
    mu = jnp.mean(y, axis=-1, keepdims=True)
    yc = y - mu
    z = yc * lax.rsqrt(jnp.mean(yc * yc, axis=-1, keepdims=True) + EPS) * lng_ref[...] + lnb_ref[...]
    z = z * jax.nn.sigmoid(z)
    o_ref[...] = x + _dot(z.astype(BF16), w2_ref[...]) + b2_ref[...]
    st_ref[...] = buf_ref[:, tt:tt + HIST_PAD, :]


def _conv_module(x, hist, g, w1, b1, wdw, bdw, lng, lnb, w2, b2, cidx, nb, nbb, t_len, tt):
    nt = t_len // tt
    row = pl.BlockSpec((nbb * tt, D_MODEL), lambda b, t: (b * nt + t, 0))
    vec = pl.BlockSpec((1, D_MODEL), lambda b, t: (0, 0))
    st = pl.BlockSpec((nbb, HIST_PAD, D_MODEL), lambda b, t: (b, 0, 0))
    return pl.pallas_call(
        functools.partial(_conv_kernel, nbb=nbb, tt=tt),
        grid=(nb // nbb, nt),
        in_specs=[row, st, vec,
                  pl.BlockSpec((None, D_MODEL, 2 * D_MODEL), lambda b, t: (cidx, 0, 0)),
                  pl.BlockSpec((1, 2 * D_MODEL), lambda b, t: (0, 0)),
                  pl.BlockSpec((HIST_PAD, D_MODEL), lambda b, t: (0, 0)),
                  vec, vec, vec,
                  pl.BlockSpec((None, D_MODEL, D_MODEL), lambda b, t: (cidx, 0, 0)),
                  vec],
        out_specs=[row, st],
        out_shape=[jax.ShapeDtypeStruct((nb * t_len, D_MODEL), F32),
                   jax.ShapeDtypeStruct((nb, HIST_PAD, D_MODEL), F32)],
        scratch_shapes=[pltpu.VMEM((nbb, HIST_PAD + tt, D_MODEL), F32)]
                       + [pltpu.VMEM((nbb, HIST_PAD - SUBLANES + tt, D_MODEL), F32)] * 2,
        compiler_params=_params(("parallel", "arbitrary")),
        name="conv_module",
    )(x, hist, g, w1, b1, wdw, bdw, lng, lnb, w2, b2)


def _head_tile_view(a):
    lead, m = a.shape[:-3], a.shape[-3]
    a = a.reshape(lead + (m, N_XHEADS, XHEAD_DIM // LANES, LANES))
    return jnp.swapaxes(a, -3, -2).reshape(lead + (m * HT_ROWS, LANES))


def _from_head_tile_view(x):
    lead, m = x.shape[:-2], x.shape[-2] // HT_ROWS
    x = x.reshape(lead + (m, XHEAD_DIM // LANES, N_XHEADS, LANES))
    return jnp.swapaxes(x, -3, -2).reshape(lead + (m, N_XHEADS, XHEAD_DIM))


def _memkv_kernel(m_ref, wk_ref, wv_ref, ok_ref, okb_ref, ov_ref, ovb_ref):
    mem = m_ref[...].astype(BF16)
    for w_ref, o_ref, ob_ref in ((wk_ref, ok_ref, okb_ref), (wv_ref, ov_ref, ovb_ref)):
        y = _dot(mem, w_ref[...])
        ob_ref[...] = y.astype(BF16)
        rows = y.shape[0]
        for hd in range(N_XHEADS):
            for c in range(XHEAD_DIM // LANES):
                col = hd * XHEAD_DIM + c * LANES
                o_ref[pl.ds(c * N_XHEADS + hd, rows, stride=HT_ROWS), :] = y[:, col:col + LANES]


def _mem_kv(mem, w_kv, tm):
    m = mem.shape[0]
    view = pl.BlockSpec((None, tm * HT_ROWS, LANES), lambda i, r: (i, r, 0))
    rowm = pl.BlockSpec((None, tm, D_MODEL), lambda i, r: (i, r, 0))
    view_sds = jax.ShapeDtypeStruct((DEPTH, m * HT_ROWS, LANES), F32)
    rowm_sds = jax.ShapeDtypeStruct((DEPTH, m, D_MODEL), BF16)
    return pl.pallas_call(
        _memkv_kernel,
        grid=(DEPTH, m // tm),
        in_specs=[pl.BlockSpec((tm, D_MODEL), lambda i, r: (r, 0)),
                  pl.BlockSpec((None, D_MODEL, D_MODEL), lambda i, r: (i, 0, 0)),
                  pl.BlockSpec((None, D_MODEL, D_MODEL), lambda i, r: (i, 0, 1))],
        out_specs=[view, rowm, view, rowm],
        out_shape=[view_sds, rowm_sds, view_sds, rowm_sds],
        compiler_params=_params(("parallel", "parallel")),
        name="mem_kv",
    )(mem, w_kv, w_kv)


def _xhead(q, mk, mv):
    s = _dot_nt(q, mk)
    p = jnp.exp(s - jnp.max(s, axis=-1, keepdims=True))
    l = jnp.sum(p, axis=-1, keepdims=True)
    return _dot(p.astype(BF16), mv) / l


def _xattn_p_kernel(x_ref, g_ref, wq_ref, mk_ref, mv_ref, wo_ref, *rest):
    o_ref = rest[-1]
    x = x_ref[...]
    if len(rest) == 3:
        a_ref, wa_ref = rest[:2]
        x = x + _dot(a_ref[...], wa_ref[...])
    h = _rms(x, g_ref[...]).astype(BF16)
    q = (_dot(h, wq_ref[...]) * (XHEAD_DIM ** -0.5)).astype(BF16)
    outs = []
    for hd in range(N_XHEADS):
        cs = slice(hd * XHEAD_DIM, (hd + 1) * XHEAD_DIM)
        outs.append(_xhead(q[:, cs], mk_ref[:, cs], mv_ref[:, cs]))
    o = jnp.concatenate(outs, axis=-1)
    o_ref[...] = x + _dot(o.astype(BF16), wo_ref[...])


def _xattn_prompt(x, g, wq, mk_b, mv_b, layer, wo, tm, pre=None):
    m = x.shape[0]
    per_b = SEQ // tm
    row = pl.BlockSpec((tm, D_MODEL), lambda i: (i, 0))
    wspec = pl.BlockSpec((None, D_MODEL, D_MODEL), lambda i: (layer, 0, 0))
    mem = pl.BlockSpec((None, N_MEM, D_MODEL), lambda i: (layer, i // per_b, 0))
    in_specs = [row, pl.BlockSpec((1, D_MODEL), lambda i: (0, 0)), wspec, mem, mem, wspec]
    args = [x, g, wq, mk_b, mv_b, wo]
    if pre is not None:
        a, wa, la = pre
        in_specs += [row, pl.BlockSpec((None, D_MODEL, D_MODEL), lambda i: (la, 0, 0))]
        args += [a, wa]
    return pl.pallas_call(
        _xattn_p_kernel,
        grid=(m // tm,),
        in_specs=in_specs,
        out_specs=row,
        out_shape=jax.ShapeDtypeStruct((m, D_MODEL), F32),
        compiler_params=_params(("parallel",)),
        name="xattn_prompt",
    )(*args)


def _xattn_s_kernel(x_ref, g_ref, wq_ref, mk_ref, mv_ref, wo_ref, o_ref, q_ref, a_ref):
    b = pl.program_id(0)

    @pl.when(b == 0)
    def _():
        h = _rms(x_ref[...], g_ref[...]).astype(BF16)
        q_ref[...] = _dot(h, wq_ref[...]) * (XHEAD_DIM ** -0.5)

    def head(ref, s, hd):
        return jnp.concatenate([ref[s, pl.ds(c * N_XHEADS + hd, N_MEM, stride=HT_ROWS), :]
                                for c in range(XHEAD_DIM // LANES)], axis=1).astype(BF16)

    rows = XS_SEQS * DEC_SEQ
    r0 = pl.multiple_of(b * rows, rows)
    q = q_ref[pl.ds(r0, rows), :]
    seqs = []
    for s in range(XS_SEQS):
        qs = q[s * DEC_SEQ:(s + 1) * DEC_SEQ].astype(BF16)
        seqs.append(jnp.concatenate(
            [_xhead(qs[:, hd * XHEAD_DIM:(hd + 1) * XHEAD_DIM], head(mk_ref, s, hd), head(mv_ref, s, hd))
             for hd in range(N_XHEADS)], axis=-1))
    a_ref[pl.ds(r0, rows), :] = jnp.concatenate(seqs, axis=0)

    @pl.when(b == pl.num_programs(0) - 1)
    def _():
        o_ref[...] = x_ref[...] + _dot(a_ref[...].astype(BF16), wo_ref[...])


def _xattn_sample(x, g, wq, mem_k, mem_v, layer, wo):
    m = x.shape[0]
    full = pl.BlockSpec((m, D_MODEL), lambda b: (0, 0))
    wspec = pl.BlockSpec((None, D_MODEL, D_MODEL), lambda b: (layer, 0, 0))
    mem = pl.BlockSpec((None, XS_SEQS, N_MEM * HT_ROWS, LANES), lambda b: (layer, b, 0, 0))
    return pl.pallas_call(
        _xattn_s_kernel,
        grid=(DEC_BATCH // XS_SEQS,),
        in_specs=[full, pl.BlockSpec((1, D_MODEL), lambda b: (0, 0)), wspec, mem, mem, wspec],
        out_specs=full,
        out_shape=jax.ShapeDtypeStruct((m, D_MODEL), F32),
        scratch_shapes=[pltpu.VMEM((m, D_MODEL), F32), pltpu.VMEM((m, D_MODEL), F32)],
        compiler_params=_params(("arbitrary",)),
        name="xattn_sample",
    )(x, g, wq, mem_k, mem_v, wo)


def _vec(v):
    return v.reshape(1, -1).astype(F32)


def kernel(x_prompt, x_sample, cache_k_diff, cache_v_diff, state_conv, cache_mem_k, cache_mem_v, page_table, mem_prompt, ffn_norm, ffn_w_in, ffn_w_out, mix_norm, diff_w_qkv, diff_lambda, diff_subln, diff_w_o, rel_bias, conv_w_pw1, conv_b_pw1, conv_w_dw, conv_b_dw, conv_ln_g, conv_ln_b, conv_w_pw2, conv_b_pw2, xattn_norm, xattn_w_q, xattn_w_kv, xattn_w_o, final_norm):
    mp = BATCH * SEQ
    ms = DEC_BATCH * DEC_SEQ
    xp = x_prompt.reshape(mp, D_MODEL)
    xs = x_sample.reshape(ms, D_MODEL)
    tm_p, tm_s = ROW_TILE, ms

    w_in = ffn_w_in.astype(BF16)
    w_out = ffn_w_out.astype(BF16)
    wqkv = diff_w_qkv.astype(BF16)
    wqkv_t = jnp.swapaxes(diff_w_qkv, 1, 2).astype(BF16)
    w_do = diff_w_o.astype(BF16)
    w_pw1 = conv_w_pw1.astype(BF16)
    w_pw2 = conv_w_pw2.astype(BF16)
    w_xq = xattn_w_q.astype(BF16)
    w_xkv = xattn_w_kv.astype(BF16)
    w_xo = xattn_w_o.astype(BF16)

    kk = jnp.arange(ATT_TK)[:, None]
    qq = jnp.arange(ATT_TQ)[None, :]
    d0 = qq - kk
    bucket_p = jnp.concatenate([jnp.where(d0 >= 0, _rel_bucket(jnp.maximum(d0, 0)), -1),
                                _rel_bucket(d0 + ATT_TK)], axis=0).astype(jnp.int32)
    bias_p = _bias_tiles(rel_bias, bucket_p, LOG2E, 2).reshape(N_HEADS, 2, ATT_TK, 2 * ATT_TQ)
    rs = jnp.arange(DEC_SEQ)[:, None]
    cs = jnp.arange(PAGE_SIZE)[None, :]
    d_new = rs - cs
    bucket_s = jnp.concatenate(
        [_rel_bucket(PAGE_SIZE + rs - cs),
         jnp.where((d_new >= 0) & (cs < DEC_SEQ), _rel_bucket(jnp.maximum(d_new, 0)), -1)],
        axis=1).astype(jnp.int32)
    bias_s = _bias_tiles(rel_bias, bucket_s)
    bias_s = jnp.broadcast_to(bias_s[:, None], (N_HEADS, 2, DEC_SEQ, 2 * PAGE_SIZE)).reshape(
        QROWS, 2 * PAGE_SIZE)

    n_pool = cache_k_diff.shape[1]
    ckt = cache_k_diff.transpose(0, 1, 3, 4, 2).reshape(-1, n_pool, D_MODEL, PAGE_SIZE)
    cv = cache_v_diff.reshape(-1, n_pool, PAGE_SIZE * N_HEADS, DV)
    pt = page_table.reshape(-1).astype(jnp.int32)

    mem2d = mem_prompt.reshape(BATCH * N_MEM, D_MODEL)
    mk, mk_b, mv, mv_b = _mem_kv(mem2d, w_xkv, 512)
    cmk = _head_tile_view(cache_mem_k)
    cmv = _head_tile_view(cache_mem_v)

    n_attn = (DEPTH + N_MIXERS - 1) // N_MIXERS
    gf = _vec(final_norm)
    kv_p = None
    conv_p, k_s, v_s, conv_s = [], [], [], []
    for i in range(DEPTH):
        g = _vec(ffn_norm[i, 0])
        xp, xs = _ffn(xp, xs, g, w_in, w_out, i, 0, FFN_TM)
        gm = _vec(mix_norm[i])
        pre = None
        if i % N_MIXERS == 0:
            a = i // N_MIXERS
            lam_init = 0.8 - 0.6 * math.exp(-0.3 * i)
            lp = diff_lambda[a].astype(F32)
            gs = _vec(diff_subln[a])
            q, k, v = _qkv_sample(xs, gm, wqkv, a)
            k_s.append(k)
            v_s.append(v)
            os_ = _attn_sample(pt, q, ckt, cv, a, k, v, bias_s, lp, gs, lam_init)
            xs = _proj_res(os_, w_do, a, xs, tm_s)
            qt, kt_all, kb, v_all, vt = _qkv_prompt(xp, gm, wqkv, wqkv_t, a, n_attn, QKV_TM, prev=kv_p)
            kv_p = (kt_all, v_all)
            op = _attn_prompt(qt, kb, vt, bias_p, lp, gs, lam_init)
            pre = (op, w_do, a)
        else:
            cidx = i // N_MIXERS
            cw = (w_pw1, _vec(conv_b_pw1[cidx]),
                  jnp.pad(conv_w_dw[cidx], ((0, HIST_PAD - CONV_W), (0, 0))), _vec(conv_b_dw[cidx]),
                  _vec(conv_ln_g[cidx]), _vec(conv_ln_b[cidx]),
                  w_pw2, _vec(conv_b_pw2[cidx]))
            pad = HIST_PAD - (CONV_W - 1)
            hist_p = jnp.zeros((BATCH, HIST_PAD, D_MODEL), F32)
            hist_s = jnp.pad(state_conv[cidx], ((0, 0), (pad, 0), (0, 0)))
            xp, st_p = _conv_module(xp, hist_p, gm, *cw, cidx, BATCH, 1, SEQ, 512)
            xs, st_s = _conv_module(xs, hist_s, gm, *cw, cidx, DEC_BATCH, DEC_BATCH, DEC_SEQ, DEC_SEQ)
            conv_p.append(st_p[:, pad:])
            conv_s.append(st_s[:, pad:])
        gx = _vec(xattn_norm[i])
        xp = _xattn_prompt(xp, gx, w_xq, mk_b, mv_b, i, w_xo, tm_p, pre=pre)
        xs = _xattn_sample(xs, gx, w_xq, cmk, cmv, i, w_xo)
        g = _vec(ffn_norm[i, 1])
        last = gf if i == DEPTH - 1 else None
        xp, xs = _ffn(xp, xs, g, w_in, w_out, i, 1, FFN_TM, final_g=last)

    y_prompt = xp.reshape(BATCH, SEQ, D_MODEL)
    y_sample = xs.reshape(DEC_BATCH, DEC_SEQ, D_MODEL)
    na = n_attn
    kt_all, v_all = kv_p
    mem_shape = (DEPTH, BATCH, N_MEM, N_XHEADS, XHEAD_DIM)
    new_k_prompt = kt_all.reshape(na, BATCH, 2 * N_HEADS, DQK, SEQ).transpose(0, 1, 4, 2, 3)
    return (y_prompt, y_sample,
            new_k_prompt,
            v_all.reshape(na, BATCH, SEQ, N_HEADS, DV),
            jnp.stack(conv_p),
            _from_head_tile_view(mk).reshape(mem_shape),
            _from_head_tile_view(mv).reshape(mem_shape),
            jnp.stack(k_s).reshape(na, DEC_BATCH, DEC_SEQ, 2 * N_HEADS, DQK),
            jnp.stack(v_s).reshape(na, DEC_BATCH, DEC_SEQ, N_HEADS, DV),
            jnp.stack(conv_s))
```
